```python
import math
import jax
import jax.numpy as jnp
from jax import lax
import numpy as np

D_MODEL = 1024
BATCH = 8
SEQ = 2048
DEPTH = 4
DEC_BATCH = 128
DEC_SEQ = 4
PAST_LEN = 8192
PAGE_SIZE = 128

N_MIXERS = 2
ROPE_THETA = 500000.0
NORM_EPS = 1e-6
A_HEADS = 8
A_NOPE = 128
A_ROPE = 64
A_DK = A_NOPE + A_ROPE
A_V = 128
A_Q_LORA = 384
A_KV_LORA = 256
A_QBLOCK = 128
B_GROUPS = ((128, 1), (512, 4), (2048, 16))
B_N_GROUPS = 3
B_HEADS = 8
B_HEAD_DIM = 128
B_ROT = B_HEAD_DIM // 4
B_BLOCK = 128
N_EXPERTS = 32
TOP_K = 4
D_FF = 1024
SWIGLU_ALPHA = 1.702
SWIGLU_LIMIT = 7.0
MOE_BLOCK = 128

kernel_name = "hybrid_mla_dilated_swa_moe_adaln_step"


def rmsnorm(x, g):
    xf = x.astype(jnp.float32)
    y = xf * lax.rsqrt(jnp.mean(xf * xf, axis=-1, keepdims=True) + NORM_EPS)
    return (y * g.astype(jnp.float32)).astype(x.dtype)


def rope(x, pos):
    half = x.shape[-1] // 2
    inv = jnp.exp(-math.log(ROPE_THETA) * jnp.arange(half, dtype=jnp.float32) / half)
    ang = pos.astype(jnp.float32)[:, None] * inv[None, :]
    shape = (1, pos.shape[0]) + (1,) * (x.ndim - 3) + (half,)
    cos = jnp.cos(ang).reshape(shape)
    sin = jnp.sin(ang).reshape(shape)
    xf = x.astype(jnp.float32)
    x1, x2 = xf[..., :half], xf[..., half:]
    return jnp.concatenate([x1 * cos - x2 * sin, x2 * cos + x1 * sin], axis=-1).astype(x.dtype)


def partial_rope(x, pos):
    return jnp.concatenate([rope(x[..., :B_ROT], pos), x[..., B_ROT:]], axis=-1)


def adaln(c, w, b):
    return jnp.split(jax.nn.silu(c) @ w + b, 6, axis=-1)


def modulate(x, g, shift, scale):
    return rmsnorm(x, g) * (1 + scale[:, None, :]) + shift[:, None, :]


def mla_queries_latents(h, pos, w_in, g_q, g_kv, w_uq, g_qn):
    z = h @ w_in
    q_lat = rmsnorm(z[..., :A_Q_LORA], g_q)
    ckv = rmsnorm(z[..., A_Q_LORA:A_Q_LORA + A_KV_LORA], g_kv)
    kpe = rope(z[..., A_Q_LORA + A_KV_LORA:], pos)
    q = (q_lat @ w_uq).reshape(h.shape[:2] + (A_HEADS, A_DK))
    q = jnp.concatenate([q[..., :A_NOPE], rope(q[..., A_NOPE:], pos)], axis=-1)
    return rmsnorm(q, g_qn), ckv, kpe


def mla_keys_values(ckv, kpe, w_ukv, g_kn):
    kv = (ckv @ w_ukv).reshape(ckv.shape[:-1] + (A_HEADS, A_NOPE + A_V))
    kpe_h = jnp.broadcast_to(kpe[..., None, :], kpe.shape[:-1] + (A_HEADS, A_ROPE))
    k = rmsnorm(jnp.concatenate([kv[..., :A_NOPE], kpe_h], axis=-1), g_kn)
    return k, kv[..., A_NOPE:]


def mla_prompt(h, pos, w_in, g_q, g_kv, w_uq, g_qn, w_ukv, g_kn, w_o):
    B, S = h.shape[:2]
    q, ckv, kpe = mla_queries_latents(h, pos, w_in, g_q, g_kv, w_uq, g_qn)
    k, v = mla_keys_values(ckv, kpe, w_ukv, g_kn)
    n_q = S // A_QBLOCK
    scale = A_DK ** -0.5
    key_pos = jnp.arange(S)

    def query_block(args):
        qb, start = args
        s = jnp.einsum('bqhd,bkhd->bhqk', qb, k, preferred_element_type=jnp.float32) * scale
        qpos = start + jnp.arange(A_QBLOCK)
        s = jnp.where(qpos[:, None] >= key_pos[None, :], s, -jnp.inf)
        p = jax.nn.softmax(s, axis=-1).astype(v.dtype)
        return jnp.einsum('bhqk,bkhd->bqhd', p, v)

    qbs = q.reshape(B, n_q, A_QBLOCK, A_HEADS, A_DK).transpose(1, 0, 2, 3, 4)
    o = lax.map(query_block, (qbs, jnp.arange(n_q) * A_QBLOCK))
    o = o.transpose(1, 0, 2, 3, 4).reshape(B, S, A_HEADS * A_V)
    return o @ w_o, ckv, kpe


def mla_sample(h, pos, cache_ckv, cache_kpe, page_table, w_in, g_q, g_kv, w_uq, g_qn, w_ukv, g_kn, w_o):
    DB, DS = h.shape[:2]
    q, ckv, kpe = mla_queries_latents(h, pos, w_in, g_q, g_kv, w_uq, g_qn)
    scale = A_DK ** -0.5

    def attend(carry, k, v, mask):
        m, l, acc = carry
        s = jnp.einsum('bqhd,bkhd->bhqk', q, k, preferred_element_type=jnp.float32) * scale
        if mask is not None:
            s = jnp.where(mask, s, -jnp.inf)
        m_new = jnp.maximum(m, jnp.max(s, axis=-1))
        corr = jnp.exp(m - m_new)
        p = jnp.exp(s - m_new[..., None])
        l = l * corr + jnp.sum(p, axis=-1)
        acc = acc * corr[..., None] + jnp.einsum('bhqk,bkhd->bhqd', p.astype(v.dtype), v,
                                                 preferred_element_type=jnp.float32)
        return (m_new, l, acc)

    def page_step(carry, phys):
        k, v = mla_keys_values(cache_ckv[phys], cache_kpe[phys], w_ukv, g_kn)
        return attend(carry, k, v, None), None

    init = (jnp.full((DB, A_HEADS, DS), -1e30, jnp.float32),
            jnp.zeros((DB, A_HEADS, DS), jnp.float32),
            jnp.zeros((DB, A_HEADS, DS, A_V), jnp.float32))
    carry, _ = lax.scan(page_step, init, page_table.T)
    k_new, v_new = mla_keys_values(ckv, kpe, w_ukv, g_kn)
    causal = jnp.tril(jnp.ones((DS, DS), dtype=bool))
    m, l, acc = attend(carry, k_new, v_new, causal)
    o = (acc / l[..., None]).transpose(0, 2, 1, 3).reshape(DB, DS, A_HEADS * A_V).astype(h.dtype)
    return o @ w_o, ckv, kpe


def b_project(h, pos, w_in, g_qn, g_kn):
    z = (h @ w_in).reshape(h.shape[:2] + (B_N_GROUPS, 3, B_HEADS, B_HEAD_DIM))
    q = partial_rope(rmsnorm(z[:, :, :, 0], g_qn[:, None, :]), pos)
    k = partial_rope(rmsnorm(z[:, :, :, 1], g_kn[:, None, :]), pos)
    return q, k, z[:, :, :, 2]


def dilated_prompt_group(q, k, v, window, dilation):
    B, S, H, hd = q.shape
    span = window // dilation
    L = S // dilation
    nb = -(-L // B_BLOCK)
    Lp = nb * B_BLOCK
    scale = hd ** -0.5

    def sub(x, front):
        x = x.reshape(B, L, dilation, H, hd).transpose(0, 2, 1, 3, 4)
        return jnp.pad(x, ((0, 0), (0, 0), (front, Lp - L), (0, 0), (0, 0)))

    def band(x):
        xs = sub(x, B_BLOCK)
        prev = xs[:, :, :Lp].reshape(B, dilation, nb, B_BLOCK, H, hd)
        cur = xs[:, :, B_BLOCK:].reshape(B, dilation, nb, B_BLOCK, H, hd)
        return jnp.concatenate([prev, cur], axis=3)

    qb = sub(q, 0).reshape(B, dilation, nb, B_BLOCK, H, hd)
    kb, vb = band(k), band(v)
    s = jnp.einsum('brnqhd,brnkhd->brnhqk', qb, kb, preferred_element_type=jnp.float32) * scale
    key_off = jnp.arange(2 * B_BLOCK) - B_BLOCK
    rel = jnp.arange(B_BLOCK)[:, None] - key_off[None, :]
    key_idx = jnp.arange(nb)[:, None] * B_BLOCK + key_off[None, :]
    mask = ((rel >= 0) & (rel <= span))[None] & (key_idx >= 0)[:, None, :]
    s = jnp.where(mask[None, None, :, None], s, -jnp.inf)
    lse = jax.nn.logsumexp(s, axis=-1)
    p = jnp.exp(s - lse[..., None]).astype(v.dtype)
    o = jnp.einsum('brnhqk,brnkhd->brnqhd', p, vb)
    o = o.reshape(B, dilation, Lp, H, hd)[:, :, :L].transpose(0, 2, 1, 3, 4).reshape(B, S, H, hd)
    lse = lse.transpose(0, 1, 2, 4, 3).reshape(B, dilation, Lp, H)[:, :, :L]
    lse = lse.transpose(0, 2, 1, 3).reshape(B, S, H)
    return o, lse


def dilated_sample_group(q, k, v, buf, window, dilation):
    DB, DS, H, hd = q.shape
    Wb = buf.shape[1]
    span = window // dilation
    scale = hd ** -0.5
    ext = jnp.concatenate([buf, jnp.stack([k, v], axis=2)], axis=1)
    idx = Wb + jnp.arange(DS)[:, None] - dilation * jnp.arange(span + 1)[None, :]
    valid = idx >= 0
    g = ext[:, jnp.maximum(idx, 0)]
    s = jnp.einsum('bqhd,bqjhd->bqhj', q, g[:, :, :, 0], preferred_element_type=jnp.float32) * scale
    s = jnp.where(valid[None, :, None, :], s, -jnp.inf)
    lse = jax.nn.logsumexp(s, axis=-1)
    p = jnp.exp(s - lse[..., None]).astype(v.dtype)
    o = jnp.einsum('bqhj,bqjhd->bqhd', p, g[:, :, :, 1])
    return o, lse, ext[:, DS:]


def combine_groups(outs, lses, w_o):
    w = jax.nn.softmax(jnp.stack(lses, axis=0).astype(jnp.float32), axis=0)
    o = jnp.sum(w[..., None].astype(outs[0].dtype) * jnp.stack(outs, axis=0), axis=0)
    return o.reshape(o.shape[:2] + (B_HEADS * B_HEAD_DIM,)) @ w_o


def dilated_prompt(h, pos, w_in, g_qn, g_kn, w_o):
    S = h.shape[1]
    q, k, v = b_project(h, pos, w_in, g_qn, g_kn)
    outs, lses, bufs = [], [], []
    for g, (window, dilation) in enumerate(B_GROUPS):
        o, lse = dilated_prompt_group(q[:, :, g], k[:, :, g], v[:, :, g], window, dilation)
        outs.append(o)
        lses.append(lse)
        keep = min(window, S)
        bufs.append(jnp.stack([k[:, :, g], v[:, :, g]], axis=2)[:, S - keep:])
    return combine_groups(outs, lses, w_o), bufs


def dilated_sample(h, pos, bufs_in, w_in, g_qn, g_kn, w_o):
    q, k, v = b_project(h, pos, w_in, g_qn, g_kn)
    outs, lses, bufs = [], [], []
    for g, (window, dilation) in enumerate(B_GROUPS):
        o, lse, nb = dilated_sample_group(q[:, :, g], k[:, :, g], v[:, :, g], bufs_in[g], window, dilation)
        outs.append(o)
        lses.append(lse)
        bufs.append(nb)
    return combine_groups(outs, lses, w_o), bufs


def moe_ffn(h, w_router, b_router, w_gu, b_gu, w_down, b_down):
    T, D = h.shape
    TK = T * TOP_K
    logits = (h @ w_router + b_router).astype(jnp.float32)
    top_val, top_idx = lax.top_k(logits, TOP_K)
    gates = jax.nn.softmax(top_val, axis=-1)
    e_flat = top_idx.reshape(TK)
    tok_flat = jnp.arange(TK, dtype=jnp.int32) // TOP_K
    order = jnp.argsort(e_flat)
    e_sorted, tok_sorted = e_flat[order], tok_flat[order]
    g_sorted = gates.reshape(TK)[order]
    counts = jnp.zeros((N_EXPERTS,), jnp.int32).at[e_flat].add(1)
    padded = (counts + MOE_BLOCK - 1) // MOE_BLOCK * MOE_BLOCK
    pad_end = jnp.cumsum(padded)
    pad_start = pad_end - padded
    start = jnp.cumsum(counts) - counts
    dest = pad_start[e_sorted] + jnp.arange(TK, dtype=jnp.int32) - start[e_sorted]
    n_blocks = -(-TK // MOE_BLOCK) + N_EXPERTS
    xs = jnp.zeros((n_blocks * MOE_BLOCK, D), h.dtype).at[dest].set(h[tok_sorted])
    block_e = jnp.minimum(jnp.searchsorted(pad_end, jnp.arange(n_blocks, dtype=jnp.int32) * MOE_BLOCK,
                                           side='right'), N_EXPERTS - 1)

    def expert_block(args):
        xb, e = args
        gu = xb @ w_gu[e] + b_gu[e]
        gate = jnp.minimum(gu[:, :D_FF], SWIGLU_LIMIT)
        up = jnp.clip(gu[:, D_FF:], -SWIGLU_LIMIT, SWIGLU_LIMIT)
        act = gate * jax.nn.sigmoid(SWIGLU_ALPHA * gate) * (up + 1)
        return act @ w_down[e] + b_down[e]

    ys = lax.map(expert_block, (xs.reshape(n_blocks, MOE_BLOCK, D), block_e))
    y_slots = ys.reshape(n_blocks * MOE_BLOCK, D)[dest] * g_sorted[:, None].astype(h.dtype)
    return jax.ops.segment_sum(y_slots, tok_sorted, num_segments=T)


def setup_inputs(seed: int = 0) -> dict:
    key = jax.random.key(seed)
    ks = iter(jax.random.split(key, 40))
    f32 = jnp.float32

    def nrm(shape, scale=1.0):
        return jax.random.normal(next(ks), shape, f32) * scale

    def gain(shape):
        return 1.0 + nrm(shape, 0.02)

    n_a = (DEPTH + 1) // 2
    n_b = DEPTH // 2
    n_pages = PAST_LEN // PAGE_SIZE
    n_used = DEC_BATCH * n_pages
    n_phys = n_used + (n_used + 3) // 4
    D = D_MODEL
    inp = {}
    inp['x_prompt'] = nrm((BATCH, SEQ, D))
    inp['x_sample'] = nrm((DEC_BATCH, DEC_SEQ, D))
    inp['cache_mla_ckv'] = nrm((n_a, n_phys, PAGE_SIZE, A_KV_LORA))
    inp['cache_mla_kpe'] = nrm((n_a, n_phys, PAGE_SIZE, A_ROPE))
    inp['state_swa_kv_w128'] = nrm((n_b, DEC_BATCH, min(128, PAST_LEN), 2, B_HEADS, B_HEAD_DIM))
    inp['state_swa_kv_w512'] = nrm((n_b, DEC_BATCH, min(512, PAST_LEN), 2, B_HEADS, B_HEAD_DIM))
    inp['state_swa_kv_w2048'] = nrm((n_b, DEC_BATCH, min(2048, PAST_LEN), 2, B_HEADS, B_HEAD_DIM))
    inp['page_table'] = jax.random.permutation(next(ks), n_phys)[:n_used].reshape(DEC_BATCH, n_pages).astype(jnp.int32)
    inp['c_prompt'] = nrm((BATCH, D))
    inp['c_sample'] = nrm((DEC_BATCH, D))
    inp['w_ada'] = nrm((DEPTH, D, 6 * D), 0.5 * D ** -0.5)
    inp['b_ada'] = nrm((DEPTH, 6 * D), 0.02)
    inp['g_norm_mix'] = gain((DEPTH, D))
    inp['g_norm_ffn'] = gain((DEPTH, D))
    inp['a_w_in'] = nrm((n_a, D, A_Q_LORA + A_KV_LORA + A_ROPE), D ** -0.5)
    inp['a_g_q'] = gain((n_a, A_Q_LORA))
    inp['a_g_kv'] = gain((n_a, A_KV_LORA))
    inp['a_w_uq'] = nrm((n_a, A_Q_LORA, A_HEADS * A_DK), A_Q_LORA ** -0.5)
    inp['a_w_ukv'] = nrm((n_a, A_KV_LORA, A_HEADS * (A_NOPE + A_V)), A_KV_LORA ** -0.5)
    inp['a_g_qn'] = gain((n_a, A_DK))
    inp['a_g_kn'] = gain((n_a, A_DK))
    inp['a_w_o'] = nrm((n_a, A_HEADS * A_V, D), (A_HEADS * A_V) ** -0.5)
    inp['b_w_in'] = nrm((n_b, D, B_N_GROUPS * 3 * B_HEADS * B_HEAD_DIM), D ** -0.5)
    inp['b_g_qn'] = gain((n_b, B_N_GROUPS, B_HEAD_DIM))
    inp['b_g_kn'] = gain((n_b, B_N_GROUPS, B_HEAD_DIM))
    inp['b_w_o'] = nrm((n_b, B_HEADS * B_HEAD_DIM, D), (B_HEADS * B_HEAD_DIM) ** -0.5)
    inp['moe_w_router'] = nrm((DEPTH, D, N_EXPERTS), D ** -0.5)
    inp['moe_b_router'] = nrm((DEPTH, N_EXPERTS), 0.01)
    inp['moe_w_gu'] = nrm((DEPTH, N_EXPERTS, D, 2 * D_FF), D ** -0.5)
    inp['moe_b_gu'] = nrm((DEPTH, N_EXPERTS, 2 * D_FF), 0.02)
    inp['moe_w_down'] = nrm((DEPTH, N_EXPERTS, D_FF, D), D_FF ** -0.5)
    inp['moe_b_down'] = nrm((DEPTH, N_EXPERTS, D), 0.02)
    return inp


def reference(x_prompt, x_sample, cache_mla_ckv, cache_mla_kpe, state_swa_kv_w128, state_swa_kv_w512,
              state_swa_kv_w2048, page_table, c_prompt, c_sample, w_ada, b_ada, g_norm_mix, g_norm_ffn,
              a_w_in, a_g_q, a_g_kv, a_w_uq, a_w_ukv, a_g_qn, a_g_kn, a_w_o,
              b_w_in, b_g_qn, b_g_kn, b_w_o,
              moe_w_router, moe_b_router, moe_w_gu, moe_b_gu, moe_w_down, moe_b_down):
    B, S, D = x_prompt.shape
    DB, DS, _ = x_sample.shape
    past_len = page_table.shape[1] * cache_mla_ckv.shape[2]
    pos_p = jnp.arange(S, dtype=jnp.int32)
    pos_s = past_len + jnp.arange(DS, dtype=jnp.int32)
    swa_states = (state_swa_kv_w128, state_swa_kv_w512, state_swa_kv_w2048)
    x_p, x_s = x_prompt, x_sample
    ckv_p, kpe_p, ckv_s, kpe_s = [], [], [], []
    swa_p, swa_s = [], []
    for i in range(DEPTH):
        sh_p, sc_p, ga_p, sh2_p, sc2_p, ga2_p = adaln(c_prompt, w_ada[i], b_ada[i])
        sh_s, sc_s, ga_s, sh2_s, sc2_s, ga2_s = adaln(c_sample, w_ada[i], b_ada[i])
        h_p = modulate(x_p, g_norm_mix[i], sh_p, sc_p)
        h_s = modulate(x_s, g_norm_mix[i], sh_s, sc_s)
        li = i // N_MIXERS
        if i % N_MIXERS == 0:
            wa = (a_w_in[li], a_g_q[li], a_g_kv[li], a_w_uq[li], a_g_qn[li], a_w_ukv[li], a_g_kn[li], a_w_o[li])
            o_p, c_p, r_p = mla_prompt(h_p, pos_p, *wa)
            o_s, c_s, r_s = mla_sample(h_s, pos_s, cache_mla_ckv[li], cache_mla_kpe[li], page_table, *wa)
            ckv_p.append(c_p)
            kpe_p.append(r_p)
            ckv_s.append(c_s)
            kpe_s.append(r_s)
        else:
            wb = (b_w_in[li], b_g_qn[li], b_g_kn[li], b_w_o[li])
            o_p, bufs_p = dilated_prompt(h_p, pos_p, *wb)
            o_s, bufs_s = dilated_sample(h_s, pos_s, [st[li] for st in swa_states], *wb)
            swa_p.append(bufs_p)
            swa_s.append(bufs_s)
        x_p = x_p + ga_p[:, None, :] * o_p
        x_s = x_s + ga_s[:, None, :] * o_s
        wm = (moe_w_router[i], moe_b_router[i], moe_w_gu[i], moe_b_gu[i], moe_w_down[i], moe_b_down[i])
        h_p = modulate(x_p, g_norm_ffn[i], sh2_p, sc2_p)
        h_s = modulate(x_s, g_norm_ffn[i], sh2_s, sc2_s)
        x_p = x_p + ga2_p[:, None, :] * moe_ffn(h_p.reshape(B * S, D), *wm).reshape(B, S, D)
        x_s = x_s + ga2_s[:, None, :] * moe_ffn(h_s.reshape(DB * DS, D), *wm).reshape(DB, DS, D)
    new_ckv_prompt = jnp.stack(ckv_p)
    new_kpe_prompt = jnp.stack(kpe_p)
    new_ckv_sample = jnp.stack(ckv_s)
    new_kpe_sample = jnp.stack(kpe_s)
    new_swa_prompt_w128 = jnp.stack([b[0] for b in swa_p])
    new_swa_prompt_w512 = jnp.stack([b[1] for b in swa_p])
    new_swa_prompt_w2048 = jnp.stack([b[2] for b in swa_p])
    new_swa_sample_w128 = jnp.stack([b[0] for b in swa_s])
    new_swa_sample_w512 = jnp.stack([b[1] for b in swa_s])
    new_swa_sample_w2048 = jnp.stack([b[2] for b in swa_s])
    return (x_p, x_s, new_ckv_prompt, new_kpe_prompt, new_ckv_sample, new_kpe_sample,
            new_swa_prompt_w128, new_swa_prompt_w512, new_swa_prompt_w2048,
            new_swa_sample_w128, new_swa_sample_w512, new_swa_sample_w2048)
```

```python
import functools
import math

import jax
import jax.numpy as jnp
from jax import lax
from jax.experimental import pallas as pl
from jax.experimental.pallas import tpu as pltpu

D_MODEL = 1024
DEPTH = 4
PAGE_SIZE = 128
N_MIXERS = 2
ROPE_THETA = 500000.0
NORM_EPS = 1e-6
A_HEADS = 8
A_NOPE = 128
A_ROPE = 64
A_DK = A_NOPE + A_ROPE
A_V = 128
A_Q_LORA = 384
A_KV_LORA = 256
A_QPAD = 256
B_GROUPS = ((128, 1), (512, 4), (2048, 16))
B_N_GROUPS = 3
B_HEADS = 8
B_HEAD_DIM = 128
B_ROT = B_HEAD_DIM // 4
B_BLOCK = 128
N_EXPERTS = 32
TOP_K = 4
D_FF = 1024
SWIGLU_ALPHA = 1.702
SWIGLU_LIMIT = 7.0

F32 = jnp.float32
BF16 = jnp.bfloat16
NEG = -1e30
V7X_VMEM_LIMIT = 56 * 1024 * 1024
MOE_TM = 512
MOE_FF_CHUNK = 512
FLASH_T = 512
LOCAL_T = 512
MLA_PAGES_PER_STEP = 16
NT = (((1,), (1,)), ((), ()))


def _cparams(sem):
    return pltpu.CompilerParams(dimension_semantics=sem, vmem_limit_bytes=V7X_VMEM_LIMIT)


def _rmsnorm(x, g):
    y = x * lax.rsqrt(jnp.mean(x * x, axis=-1, keepdims=True) + NORM_EPS)
    return y * g


def _rope(x, pos):
    half = x.shape[-1] // 2
    inv = jnp.exp(-math.log(ROPE_THETA) * jnp.arange(half, dtype=F32) / half)
    ang = pos.astype(F32)[:, None] * inv[None, :]
    shape = (1, pos.shape[0]) + (1,) * (x.ndim - 3) + (half,)
    cos = jnp.cos(ang).reshape(shape)
    sin = jnp.sin(ang).reshape(shape)
    x1, x2 = x[..., :half], x[..., half:]
    return jnp.concatenate([x1 * cos - x2 * sin, x2 * cos + x1 * sin], axis=-1)


def _partial_rope(x, pos):
    return jnp.concatenate([_rope(x[..., :B_ROT], pos), x[..., B_ROT:]], axis=-1)


def _modulate(x, g, shift, scale):
    return _rmsnorm(x, g) * (1 + scale[:, None, :]) + shift[:, None, :]


def _mla_queries_latents(h, pos, w_in, g_q, g_kv, w_uq, g_qn):
    z = h @ w_in
    q_lat = _rmsnorm(z[..., :A_Q_LORA], g_q)
    ckv = _rmsnorm(z[..., A_Q_LORA:A_Q_LORA + A_KV_LORA], g_kv)
    kpe = _rope(z[..., A_Q_LORA + A_KV_LORA:], pos)
    q = (q_lat @ w_uq).reshape(h.shape[:2] + (A_HEADS, A_DK))
    q = jnp.concatenate([q[..., :A_NOPE], _rope(q[..., A_NOPE:], pos)], axis=-1)
    return _rmsnorm(q, g_qn), ckv, kpe


def _mla_keys_values(ckv, kpe, w_ukv, g_kn):
    kv = (ckv @ w_ukv).reshape(ckv.shape[:-1] + (A_HEADS, A_NOPE + A_V))
    kpe_h = jnp.broadcast_to(kpe[..., None, :], kpe.shape[:-1] + (A_HEADS, A_ROPE))
    k = _rmsnorm(jnp.concatenate([kv[..., :A_NOPE], kpe_h], axis=-1), g_kn)
    return k, kv[..., A_NOPE:]


def _b_project(h, pos, w_in, g_qn, g_kn):
    z = (h @ w_in).reshape(h.shape[:2] + (B_N_GROUPS, 3, B_HEADS, B_HEAD_DIM))
    q = _partial_rope(_rmsnorm(z[:, :, :, 0], g_qn[:, None, :]), pos)
    k = _partial_rope(_rmsnorm(z[:, :, :, 1], g_kn[:, None, :]), pos)
    return q, k, z[:, :, :, 2]


def _flash_kernel(q_ref, k_ref, v_ref, o_ref):
    i = pl.program_id(2)
    q = q_ref[...]

    def update(carry, kj, vj, mask):
        m, l, acc = carry
        s = lax.dot_general(q, kj, NT, preferred_element_type=F32)
        if mask is not None:
            s = jnp.where(mask, s, NEG)
        m_new = jnp.maximum(m, jnp.max(s, axis=-1, keepdims=True))
        p = jnp.exp(s - m_new)
        corr = jnp.exp(m - m_new)
        l = l * corr + jnp.sum(p, axis=-1, keepdims=True)
        acc = acc * corr + jnp.dot(p.astype(BF16), vj, preferred_element_type=F32)
        return m_new, l, acc

    def body(j, carry):
        start = pl.multiple_of(j * FLASH_T, FLASH_T)
        return update(carry, k_ref[pl.ds(start, FLASH_T), :], v_ref[pl.ds(start, FLASH_T), :], None)

    init = (jnp.full((FLASH_T, 1), NEG, F32), jnp.zeros((FLASH_T, 1), F32), jnp.zeros((FLASH_T, A_V), F32))
    carry = lax.fori_loop(0, i, body, init)
    start = pl.multiple_of(i * FLASH_T, FLASH_T)
    row = lax.broadcasted_iota(jnp.int32, (FLASH_T, FLASH_T), 0)
    col = lax.broadcasted_iota(jnp.int32, (FLASH_T, FLASH_T), 1)
    m, l, acc = update(carry, k_ref[pl.ds(start, FLASH_T), :], v_ref[pl.ds(start, FLASH_T), :], row >= col)
    o_ref[...] = (acc / l).astype(o_ref.dtype)


def _flash_causal(q, k, v, batch, seq):
    nq = seq // FLASH_T
    return pl.pallas_call(
        _flash_kernel,
        grid=(batch, A_HEADS, nq),
        in_specs=[pl.BlockSpec((FLASH_T, A_QPAD), lambda b, h, i: (b * nq + i, h)),
                  pl.BlockSpec((seq, A_QPAD), lambda b, h, i: (b, h)),
                  pl.BlockSpec((seq, A_V), lambda b, h, i: (b, h))],
        out_specs=pl.BlockSpec((FLASH_T, A_V), lambda b, h, i: (b * nq + i, h)),
        out_shape=jax.ShapeDtypeStruct((batch * seq, A_HEADS * A_V), BF16),
        compiler_params=_cparams(("parallel", "parallel", "arbitrary")),
        name="mla_prompt_flash",
    )(q, k, v)


def _mla_sample_kernel(pt_ref, *refs):
    npg = MLA_PAGES_PER_STEP
    ckv_refs = refs[:npg]
    kpe_refs = refs[npg:2 * npg]
    (qabs_ref, qpe_ref, ckn_ref, kpn_ref, wukt_ref, wuv_ref, o_ref,
     ck_s, kp_s, s_s, m_s, l_s, acc_s) = refs[2 * npg:]
    c = pl.program_id(1)
    nq = qabs_ref.shape[0]

    @pl.when(c == 0)
    def _():
        m_s[...] = jnp.full(m_s.shape, NEG, F32)
        l_s[...] = jnp.zeros(l_s.shape, F32)
        acc_s[...] = jnp.zeros(acc_s.shape, F32)

    qa = qabs_ref[...]
    qp = qpe_ref[...]
    wukt = wukt_ref[...]
    ones = jnp.ones((A_HEADS, A_ROPE), BF16)

    def scores(ck, kp):
        n = ck.shape[0]
        kt = lax.dot_general(wukt, ck, NT, preferred_element_type=F32)
        ssn = jnp.sum((kt * kt).reshape(A_HEADS, A_NOPE, n), axis=1)
        sq = kp * kp
        hi = sq.astype(BF16)
        lo = (sq - hi.astype(F32)).astype(BF16)
        ssp = (lax.dot_general(ones, hi, NT, preferred_element_type=F32)
               + lax.dot_general(ones, lo, NT, preferred_element_type=F32))
        r = lax.rsqrt((ssn + ssp) * (1.0 / A_DK) + NORM_EPS)
        s = (lax.dot_general(qa, ck, NT, preferred_element_type=F32)
             + lax.dot_general(qp, kp.astype(BF16), NT, preferred_element_type=F32))
        return s * jnp.concatenate([r] * (nq // A_HEADS), axis=0)

    def update(s, vals):
        m = m_s[...]
        m_new = jnp.maximum(m, jnp.max(s, axis=-1, keepdims=True))
        p = jnp.exp(s - m_new)
        corr = jnp.exp(m - m_new)
        l_s[...] = l_s[...] * corr + jnp.sum(p, axis=-1, keepdims=True)
        acc_s[...] = acc_s[...] * corr + jnp.dot(p.astype(BF16), vals, preferred_element_type=F32)
        m_s[...] = m_new

    for j in range(npg):
        ck_s[j * PAGE_SIZE:(j + 1) * PAGE_SIZE, :] = ckv_refs[j][...].astype(BF16)
        kp_s[j * PAGE_SIZE:(j + 1) * PAGE_SIZE, :] = kpe_refs[j][...]
    sub = 2 * PAGE_SIZE
    for t in range(npg // 2):
        s_s[:, t * sub:(t + 1) * sub] = scores(ck_s[t * sub:(t + 1) * sub, :], kp_s[t * sub:(t + 1) * sub, :])
    update(s_s[...], ck_s[...])

    @pl.when(c == pl.num_programs(1) - 1)
    def _():
        ckn = ckn_ref[...].astype(BF16)
        s = scores(ckn, kpn_ref[...])
        qi = lax.broadcasted_iota(jnp.int32, s.shape, 0) // A_HEADS
        ti = lax.broadcasted_iota(jnp.int32, s.shape, 1)
        update(jnp.where(ti <= qi, s, NEG), ckn)
        lat = (acc_s[...] / l_s[...]).astype(BF16)
        full = jnp.dot(lat, wuv_ref[...], preferred_element_type=F32)
        hrow = lax.broadcasted_iota(jnp.int32, (A_HEADS, A_HEADS * A_V), 0)
        hcol = lax.broadcasted_iota(jnp.int32, (A_HEADS, A_HEADS * A_V), 1) // A_V
        rows = []
        for qq in range(nq // A_HEADS):
            blk = full[qq * A_HEADS:(qq + 1) * A_HEADS, :]
            rows.append(jnp.sum(jnp.where(hrow == hcol, blk, 0.0), axis=0, keepdims=True))
        o_ref[...] = jnp.concatenate(rows, axis=0)


def _mla_sample_attention(page_table, cache_ckv, cache_kpe, li, qabs, qpe, ckv_new, kpe_new, wukt, wuv):
    db, n_pages = page_table.shape
    ds = qabs.shape[0] // (db * A_HEADS)
    nq = ds * A_HEADS
    npg = MLA_PAGES_PER_STEP
    n_chunks = n_pages // npg
    tok = npg * PAGE_SIZE

    def page_map(j):
        return lambda b, c, pt: (li, pt[b * n_pages + c * npg + j], 0, 0)

    in_specs = ([pl.BlockSpec((None, None, PAGE_SIZE, A_KV_LORA), page_map(j)) for j in range(npg)]
                + [pl.BlockSpec((None, None, PAGE_SIZE, A_ROPE), page_map(j)) for j in range(npg)]
                + [pl.BlockSpec((nq, A_KV_LORA), lambda b, c, pt: (b, 0)),
                   pl.BlockSpec((nq, A_ROPE), lambda b, c, pt: (b, 0)),
                   pl.BlockSpec((None, PAGE_SIZE, A_KV_LORA), lambda b, c, pt: (b, 0, 0)),
                   pl.BlockSpec((None, PAGE_SIZE, A_ROPE), lambda b, c, pt: (b, 0, 0)),
                   pl.BlockSpec((A_HEADS * A_NOPE, A_KV_LORA), lambda b, c, pt: (0, 0)),
                   pl.BlockSpec((A_KV_LORA, A_HEADS * A_V), lambda b, c, pt: (0, 0))])
    grid_spec = pltpu.PrefetchScalarGridSpec(
        num_scalar_prefetch=1,
        grid=(db, n_chunks),
        in_specs=in_specs,
        out_specs=pl.BlockSpec((None, ds, A_HEADS * A_V), lambda b, c, pt: (b, 0, 0)),
        scratch_shapes=[pltpu.VMEM((tok, A_KV_LORA), BF16),
                        pltpu.VMEM((tok, A_ROPE), F32),
                        pltpu.VMEM((nq, tok), F32),
                        pltpu.VMEM((nq, 1), F32),
                        pltpu.VMEM((nq, 1), F32),
                        pltpu.VMEM((nq, A_KV_LORA), F32)])
    return pl.pallas_call(
        _mla_sample_kernel,
        grid_spec=grid_spec,
        out_shape=jax.ShapeDtypeStruct((db, ds, A_HEADS * A_V), F32),
        compiler_params=_cparams(("parallel", "arbitrary")),
        name="mla_sample_paged",
    )(page_table.reshape(-1), *([cache_ckv] * npg), *([cache_kpe] * npg),
      qabs, qpe, ckv_new, kpe_new, wukt, wuv)


def _local_kernel(q_ref, k_ref, kh_ref, v_ref, vh_ref, o_ref, lse_ref, *, blocks_per_seq):
    i = pl.program_id(0)
    nsub = LOCAL_T // B_BLOCK
    qi = lax.broadcasted_iota(jnp.int32, (B_BLOCK, 2 * B_BLOCK), 0)
    ki = lax.broadcasted_iota(jnp.int32, (B_BLOCK, 2 * B_BLOCK), 1)
    rel = qi - (ki - B_BLOCK)
    band = (rel >= 0) & (rel <= B_BLOCK)
    lane = lax.broadcasted_iota(jnp.int32, (B_BLOCK, 128), 1)
    for j in range(nsub):
        if j == 0:
            has_prev = (i * nsub) % blocks_per_seq != 0
        else:
            has_prev = (j % blocks_per_seq) != 0 if blocks_per_seq <= nsub else True
        mask = band & (ki >= jnp.where(has_prev, 0, B_BLOCK))
        lse_tile = jnp.zeros((B_BLOCK, 128), F32)
        for h in range(B_HEADS):
            cs = slice(h * B_HEAD_DIM, (h + 1) * B_HEAD_DIM)
            q = q_ref[j * B_BLOCK:(j + 1) * B_BLOCK, cs]
            if j == 0:
                kk = jnp.concatenate([kh_ref[:, cs], k_ref[0:B_BLOCK, cs]], axis=0)
                vv = jnp.concatenate([vh_ref[:, cs], v_ref[0:B_BLOCK, cs]], axis=0)
            else:
                kk = k_ref[(j - 1) * B_BLOCK:(j + 1) * B_BLOCK, cs]
                vv = v_ref[(j - 1) * B_BLOCK:(j + 1) * B_BLOCK, cs]
            s = lax.dot_general(q, kk, NT, preferred_element_type=F32)
            s = jnp.where(mask, s, NEG)
            m = jnp.max(s, axis=-1, keepdims=True)
            p = jnp.exp(s - m)
            l = jnp.sum(p, axis=-1, keepdims=True)
            o = jnp.dot(p.astype(BF16), vv, preferred_element_type=F32) / l
            o_ref[j * B_BLOCK:(j + 1) * B_BLOCK, cs] = o.astype(o_ref.dtype)
            lse_tile = jnp.where(lane == h, m + jnp.log(l), lse_tile)
        lse_ref[j * B_BLOCK:(j + 1) * B_BLOCK, :] = lse_tile


def _local_attention(q, k, v, seq_len):
    rows, width = q.shape
    nsub = LOCAL_T // B_BLOCK
    cur = pl.BlockSpec((LOCAL_T, width), lambda i: (i, 0))
    halo = pl.BlockSpec((B_BLOCK, width), lambda i: (jnp.maximum(i * nsub - 1, 0), 0))
    return pl.pallas_call(
        functools.partial(_local_kernel, blocks_per_seq=seq_len // B_BLOCK),
        grid=(rows // LOCAL_T,),
        in_specs=[cur, cur, halo, cur, halo],
        out_specs=[pl.BlockSpec((LOCAL_T, width), lambda i: (i, 0)),
                   pl.BlockSpec((LOCAL_T, 128), lambda i: (i, 0))],
        out_shape=[jax.ShapeDtypeStruct((rows, width), BF16),
                   jax.ShapeDtypeStruct((rows, 128), F32)],
        compiler_params=_cparams(("parallel",)),
        name="dilated_local_attention",
    )(q, k, k, v, v)


def _moe_kernel(be_ref, nu_ref, xs_ref, wgu_ref, bgu_ref, wd_ref, bd_ref, ys_ref, wgu_bf, wd_bf):
    i = pl.program_id(0)
    e = be_ref[i]
    prev = be_ref[jnp.maximum(i - 1, 0)]

    @pl.when((i == 0) | (e != prev))
    def _():
        wgu_bf[...] = wgu_ref[...].astype(BF16)
        wd_bf[...] = wd_ref[...].astype(BF16)

    @pl.when(i < nu_ref[0])
    def _():
        x = xs_ref[...]
        acc = jnp.zeros((MOE_TM, D_MODEL), F32) + bd_ref[...]
        for c in range(D_FF // MOE_FF_CHUNK):
            lo = c * MOE_FF_CHUNK
            g = jnp.dot(x, wgu_bf[:, lo:lo + MOE_FF_CHUNK], preferred_element_type=F32)
            g = jnp.minimum(g + bgu_ref[:, lo:lo + MOE_FF_CHUNK], SWIGLU_LIMIT)
            u = jnp.dot(x, wgu_bf[:, D_FF + lo:D_FF + lo + MOE_FF_CHUNK], preferred_element_type=F32)
            u = jnp.clip(u + bgu_ref[:, D_FF + lo:D_FF + lo + MOE_FF_CHUNK], -SWIGLU_LIMIT, SWIGLU_LIMIT)
            a = g * jax.nn.sigmoid(SWIGLU_ALPHA * g) * (u + 1.0)
            acc = acc + jnp.dot(a.astype(BF16), wd_bf[lo:lo + MOE_FF_CHUNK, :], preferred_element_type=F32)
        ys_ref[...] = acc.astype(ys_ref.dtype)

    @pl.when(i >= nu_ref[0])
    def _():
        ys_ref[...] = jnp.zeros(ys_ref.shape, ys_ref.dtype)


def _moe_experts(xs, block_e, n_used, w_gu, b_gu, w_down, b_down):
    n_blocks = xs.shape[0] // MOE_TM
    grid_spec = pltpu.PrefetchScalarGridSpec(
        num_scalar_prefetch=2,
        grid=(n_blocks,),
        in_specs=[pl.BlockSpec((MOE_TM, D_MODEL), lambda i, be, nu: (i, 0)),
                  pl.BlockSpec((None, D_MODEL, 2 * D_FF), lambda i, be, nu: (be[i], 0, 0)),
                  pl.BlockSpec((None, 1, 2 * D_FF), lambda i, be, nu: (be[i], 0, 0)),
                  pl.BlockSpec((None, D_FF, D_MODEL), lambda i, be, nu: (be[i], 0, 0)),
                  pl.BlockSpec((None, 1, D_MODEL), lambda i, be, nu: (be[i], 0, 0))],
        out_specs=pl.BlockSpec((MOE_TM, D_MODEL), lambda i, be, nu: (i, 0)),
        scratch_shapes=[pltpu.VMEM((D_MODEL, 2 * D_FF), BF16),
                        pltpu.VMEM((D_FF, D_MODEL), BF16)])
    return pl.pallas_call(
        _moe_kernel,
        grid_spec=grid_spec,
        out_shape=jax.ShapeDtypeStruct(xs.shape, BF16),
        compiler_params=_cparams(("arbitrary",)),
        name="moe_experts",
    )(block_e, n_used, xs, w_gu, b_gu.reshape(N_EXPERTS, 1, 2 * D_FF), w_down,
      b_down.reshape(N_EXPERTS, 1, D_MODEL))


def _moe_ffn(h, w_router, b_router, w_gu, b_gu, w_down, b_down):
    t = h.shape[0]
    tk = t * TOP_K
    logits = h @ w_router + b_router
    top_val, top_idx = lax.top_k(logits, TOP_K)
    gates = jax.nn.softmax(top_val, axis=-1)
    e_flat = top_idx.reshape(tk)
    onehot = (e_flat[:, None] == jnp.arange(N_EXPERTS, dtype=jnp.int32)[None, :]).astype(jnp.int32)
    csum = jnp.cumsum(onehot, axis=0)
    rank = jnp.sum(onehot * csum, axis=1) - 1
    counts = csum[-1]
    padded = (counts + MOE_TM - 1) // MOE_TM * MOE_TM
    pad_end = jnp.cumsum(padded)
    pad_start = pad_end - padded
    dest = pad_start[e_flat] + rank
    n_blocks = -(-tk // MOE_TM) + N_EXPERTS
    n_used = (pad_end[-1] // MOE_TM).astype(jnp.int32)
    slot_tok = jnp.zeros((n_blocks * MOE_TM,), jnp.int32).at[dest].set(jnp.arange(tk, dtype=jnp.int32) // TOP_K)
    xs = h.astype(BF16)[slot_tok]
    blk = jnp.minimum(jnp.arange(n_blocks, dtype=jnp.int32), n_used - 1) * MOE_TM
    block_e = jnp.minimum(jnp.searchsorted(pad_end, blk, side='right'), N_EXPERTS - 1).astype(jnp.int32)
    ys = _moe_experts(xs, block_e, n_used.reshape(1), w_gu, b_gu, w_down, b_down)
    y = ys[dest].astype(F32).reshape(t, TOP_K, D_MODEL) * gates[:, :, None]
    return jnp.sum(y, axis=1)


def _mla_prompt(h, pos, w_in, g_q, g_kv, w_uq, g_qn, w_ukv, g_kn, w_o):
    b, s = h.shape[:2]
    q, ckv, kpe = _mla_queries_latents(h, pos, w_in, g_q, g_kv, w_uq, g_qn)
    k, v = _mla_keys_values(ckv, kpe, w_ukv, g_kn)
    pad = ((0, 0), (0, 0), (0, 0), (0, A_QPAD - A_DK))
    qp = jnp.pad(q * (A_DK ** -0.5), pad).astype(BF16).reshape(b * s, A_HEADS * A_QPAD)
    kp = jnp.pad(k, pad).astype(BF16).reshape(b * s, A_HEADS * A_QPAD)
    vp = v.astype(BF16).reshape(b * s, A_HEADS * A_V)
    o = _flash_causal(qp, kp, vp, b, s).reshape(b, s, A_HEADS * A_V)
    return o.astype(F32) @ w_o, ckv, kpe


def _mla_sample(h, pos, cache_ckv, cache_kpe, li, page_table, w_in, g_q, g_kv, w_uq, g_qn, w_ukv, g_kn, w_o):
    db, ds = h.shape[:2]
    q, ckv, kpe = _mla_queries_latents(h, pos, w_in, g_q, g_kv, w_uq, g_qn)
    w3 = w_ukv.reshape(A_KV_LORA, A_HEADS, A_NOPE + A_V)
    w_uk = w3[:, :, :A_NOPE]
    w_uv = w3[:, :, A_NOPE:].reshape(A_KV_LORA, A_HEADS * A_V)
    qs = q * (g_kn * (A_DK ** -0.5))
    qabs = jnp.einsum('bqhd,chd->bqhc', qs[..., :A_NOPE], w_uk).astype(BF16)
    qabs = qabs.reshape(db * ds * A_HEADS, A_KV_LORA)
    qpe = qs[..., A_NOPE:].astype(BF16).reshape(db * ds * A_HEADS, A_ROPE)
    padn = ((0, 0), (0, PAGE_SIZE - ds), (0, 0))
    wukt = w_uk.transpose(1, 2, 0).reshape(A_HEADS * A_NOPE, A_KV_LORA).astype(BF16)
    o = _mla_sample_attention(page_table, cache_ckv, cache_kpe, li, qabs, qpe,
                              jnp.pad(ckv, padn), jnp.pad(kpe, padn), wukt, w_uv.astype(BF16))
    return o @ w_o, ckv, kpe


def _combine_groups(outs, lses, w_o):
    w = jax.nn.softmax(jnp.stack(lses, axis=0), axis=0)
    o = jnp.sum(w[..., None] * jnp.stack(outs, axis=0), axis=0)
    return o.reshape(o.shape[:2] + (B_HEADS * B_HEAD_DIM,)) @ w_o


def _dilated_prompt(h, pos, w_in, g_qn, g_kn, w_o):
    b, s = h.shape[:2]
    q, k, v = _b_project(h, pos, w_in, g_qn, g_kn)
    width = B_HEADS * B_HEAD_DIM
    outs, lses, bufs = [], [], []
    for g, (window, dilation) in enumerate(B_GROUPS):
        sub_len = s // dilation

        def split(x):
            x = x.reshape(b, sub_len, dilation, width).transpose(0, 2, 1, 3)
            return x.reshape(b * s, width).astype(BF16)

        o, lse = _local_attention(split(q[:, :, g] * (B_HEAD_DIM ** -0.5)), split(k[:, :, g]), split(v[:, :, g]),
                                  sub_len)
        o = o.astype(F32).reshape(b, dilation, sub_len, width).transpose(0, 2, 1, 3)
        lse = lse[:, :B_HEADS].reshape(b, dilation, sub_len, B_HEADS).transpose(0, 2, 1, 3)
        outs.append(o.reshape(b, s, B_HEADS, B_HEAD_DIM))
        lses.append(lse.reshape(b, s, B_HEADS))
        keep = min(window, s)
        bufs.append(jnp.stack([k[:, :, g], v[:, :, g]], axis=2)[:, s - keep:])
    return _combine_groups(outs, lses, w_o), bufs


def _dilated_sample_group(q, k, v, buf, dilation):
    db, ds = q.shape[:2]
    scale = B_HEAD_DIM ** -0.5
    new = jnp.stack([k, v], axis=2)
    if dilation == 1:
        ext = jnp.concatenate([buf, new], axis=1)
        s = jnp.einsum('bqhd,bkhd->bqhk', q, ext[:, :, 0], preferred_element_type=F32) * scale
        rel = jnp.arange(ext.shape[1])[None, :] - jnp.arange(ds)[:, None]
        s = jnp.where(((rel >= 0) & (rel <= B_BLOCK))[None, :, None, :], s, -jnp.inf)
        lse = jax.nn.logsumexp(s, axis=-1)
        p = jnp.exp(s - lse[..., None])
        o = jnp.einsum('bqhk,bkhd->bqhd', p, ext[:, :, 1])
    else:
        rows = buf.reshape(db, B_BLOCK, dilation, 2, B_HEADS, B_HEAD_DIM)[:, :, :ds]
        s_old = jnp.einsum('bqhd,bmqhd->bqhm', q, rows[:, :, :, 0], preferred_element_type=F32) * scale
        s_self = jnp.sum(q * k, axis=-1, keepdims=True) * scale
        s = jnp.concatenate([s_old, s_self], axis=-1)
        lse = jax.nn.logsumexp(s, axis=-1)
        p = jnp.exp(s - lse[..., None])
        o = jnp.einsum('bqhm,bmqhd->bqhd', p[..., :B_BLOCK], rows[:, :, :, 1]) + p[..., B_BLOCK:] * v
    return o, lse, jnp.concatenate([buf[:, ds:], new], axis=1)


def _dilated_sample(h, pos, bufs_in, w_in, g_qn, g_kn, w_o):
    q, k, v = _b_project(h, pos, w_in, g_qn, g_kn)
    outs, lses, bufs = [], [], []
    for g, (window, dilation) in enumerate(B_GROUPS):
        o, lse, nb = _dilated_sample_group(q[:, :, g], k[:, :, g], v[:, :, g], bufs_in[g], dilation)
        outs.append(o)
        lses.append(lse)
        bufs.append(nb)
    return _combine_groups(outs, lses, w_o), bufs


def kernel(x_prompt, x_sample, cache_mla_ckv, cache_mla_kpe, state_swa_kv_w128, state_swa_kv_w512, state_swa_kv_w2048, page_table, c_prompt, c_sample, w_ada, b_ada, g_norm_mix, g_norm_ffn, a_w_in, a_g_q, a_g_kv, a_w_uq, a_w_ukv, a_g_qn, a_g_kn, a_w_o, b_w_in, b_g_qn, b_g_kn, b_w_o, moe_w_router, moe_b_router, moe_w_gu, moe_b_gu, moe_w_down, moe_b_down):
    B, S, D = x_prompt.shape
    DB, DS, _ = x_sample.shape
    past_len = page_table.shape[1] * cache_mla_ckv.shape[2]
    pos_p = jnp.arange(S, dtype=jnp.int32)
    pos_s = past_len + jnp.arange(DS, dtype=jnp.int32)
    swa_states = (state_swa_kv_w128, state_swa_kv_w512, state_swa_kv_w2048)
    x_p, x_s = x_prompt, x_sample
    ckv_p, kpe_p, ckv_s, kpe_s = [], [], [], []
    swa_p, swa_s = [], []
    for i in range(DEPTH):
        mod_p = jnp.split(jax.nn.silu(c_prompt) @ w_ada[i] + b_ada[i], 6, axis=-1)
        mod_s = jnp.split(jax.nn.silu(c_sample) @ w_ada[i] + b_ada[i], 6, axis=-1)
        sh_p, sc_p, ga_p, sh2_p, sc2_p, ga2_p = mod_p
        sh_s, sc_s, ga_s, sh2_s, sc2_s, ga2_s = mod_s
        h_p = _modulate(x_p, g_norm_mix[i], sh_p, sc_p)
        h_s = _modulate(x_s, g_norm_mix[i], sh_s, sc_s)
        li = i // N_MIXERS
        if i % N_MIXERS == 0:
            wa = (a_w_in[li], a_g_q[li], a_g_kv[li], a_w_uq[li], a_g_qn[li], a_w_ukv[li], a_g_kn[li], a_w_o[li])
            o_p, c_p, r_p = _mla_prompt(h_p, pos_p, *wa)
            o_s, c_s, r_s = _mla_sample(h_s, pos_s, cache_mla_ckv, cache_mla_kpe, li, page_table, *wa)
            ckv_p.append(c_p)
            kpe_p.append(r_p)
            ckv_s.append(c_s)
            kpe_s.append(r_s)
        else:
            wb = (b_w_in[li], b_g_qn[li], b_g_kn[li], b_w_o[li])
            o_p, bufs_p = _dilated_prompt(h_p, pos_p, *wb)
            o_s, bufs_s = _dilated_sample(h_s, pos_s, [st[li] for st in swa_states], *wb)
            swa_p.append(bufs_p)
            swa_s.append(bufs_s)
        x_p = x_p + ga_p[:, None, :] * o_p
        x_s = x_s + ga_s[:, None, :] * o_s
        wm = (moe_w_router[i], moe_b_router[i], moe_w_gu[i], moe_b_gu[i], moe_w_down[i], moe_b_down[i])
        h_p = _modulate(x_p, g_norm_ffn[i], sh2_p, sc2_p)
        h_s = _modulate(x_s, g_norm_ffn[i], sh2_s, sc2_s)
        y = _moe_ffn(jnp.concatenate([h_p.reshape(B * S, D), h_s.reshape(DB * DS, D)], axis=0), *wm)
        x_p = x_p + ga2_p[:, None, :] * y[:B * S].reshape(B, S, D)
        x_s = x_s + ga2_s[:, None, :] * y[B * S:].reshape(DB, DS, D)
    return (x_p, x_s, jnp.stack(ckv_p), jnp.stack(kpe_p), jnp.stack(ckv_s), jnp.stack(kpe_s),
            jnp.stack([b[0] for b in swa_p]), jnp.stack([b[1] for b in swa_p]), jnp.stack([b[2] for b in swa_p]),
            jnp.stack([b[0] for b in swa_s]), jnp.stack([b[1] for b in swa_s]), jnp.stack([b[2] for b in swa_s]))
```

```python
import functools
import math

import jax
import jax.numpy as jnp
from jax import lax
from jax.experimental import pallas as pl
from jax.experimental.pallas import tpu as pltpu

D_MODEL = 1024
DEPTH = 4
PAGE_SIZE = 128
N_MIXERS = 2
ROPE_THETA = 500000.0
NORM_EPS = 1e-6
A_HEADS = 8
A_NOPE = 128
A_ROPE = 64
A_DK = A_NOPE + A_ROPE
A_V = 128
A_Q_LORA = 384
A_KV_LORA = 256
A_QPAD = 256
B_GROUPS = ((128, 1), (512, 4), (2048, 16))
B_N_GROUPS = 3
B_HEADS = 8
B_HEAD_DIM = 128
B_ROT = B_HEAD_DIM // 4
B_BLOCK = 128
N_EXPERTS = 32
TOP_K = 4
D_FF = 1024
SWIGLU_ALPHA = 1.702
SWIGLU_LIMIT = 7.0

F32 = jnp.float32
BF16 = jnp.bfloat16
NEG = -1e30
V7X_VMEM_LIMIT = 56 * 1024 * 1024
MOE_TM = 512
MOE_FF_CHUNK = 512
ROUTER_LANES = 128
RANK_T = 1024
COMBINE_TM = 512
PROJ_TM = 512
SWA_COPY_SPLITS = 4
FLASH_T = 512
LOCAL_T = 512
MLA_PAGES_PER_STEP = 16
NT = (((1,), (1,)), ((), ()))


def _cparams(sem):
    return pltpu.CompilerParams(dimension_semantics=sem, vmem_limit_bytes=V7X_VMEM_LIMIT)


def _rmsnorm(x, g):
    y = x * lax.rsqrt(jnp.mean(x * x, axis=-1, keepdims=True) + NORM_EPS)
    return y * g


def _rope(x, pos):
    half = x.shape[-1] // 2
    inv = jnp.exp(-math.log(ROPE_THETA) * jnp.arange(half, dtype=F32) / half)
    ang = pos.astype(F32)[:, None] * inv[None, :]
    shape = (1, pos.shape[0]) + (1,) * (x.ndim - 3) + (half,)
    cos = jnp.cos(ang).reshape(shape)
    sin = jnp.sin(ang).reshape(shape)
    x1, x2 = x[..., :half], x[..., half:]
    return jnp.concatenate([x1 * cos - x2 * sin, x2 * cos + x1 * sin], axis=-1)


def _partial_rope(x, pos):
    return jnp.concatenate([_rope(x[..., :B_ROT], pos), x[..., B_ROT:]], axis=-1)


def _modulate(x, g, shift, scale):
    return _rmsnorm(x, g) * (1 + scale[:, None, :]) + shift[:, None, :]


def _mla_queries_latents(h, pos, w_in, g_q, g_kv, w_uq, g_qn):
    z = h @ w_in
    q_lat = _rmsnorm(z[..., :A_Q_LORA], g_q)
    ckv = _rmsnorm(z[..., A_Q_LORA:A_Q_LORA + A_KV_LORA], g_kv)
    kpe = _rope(z[..., A_Q_LORA + A_KV_LORA:], pos)
    q = (q_lat @ w_uq).reshape(h.shape[:2] + (A_HEADS, A_DK))
    q = jnp.concatenate([q[..., :A_NOPE], _rope(q[..., A_NOPE:], pos)], axis=-1)
    return _rmsnorm(q, g_qn), ckv, kpe


def _mla_keys_values(ckv, kpe, w_ukv, g_kn):
    kv = (ckv @ w_ukv).reshape(ckv.shape[:-1] + (A_HEADS, A_NOPE + A_V))
    kpe_h = jnp.broadcast_to(kpe[..., None, :], kpe.shape[:-1] + (A_HEADS, A_ROPE))
    k = _rmsnorm(jnp.concatenate([kv[..., :A_NOPE], kpe_h], axis=-1), g_kn)
    return k, kv[..., A_NOPE:]


def _b_project(h, pos, w_in, g_qn, g_kn):
    z = (h @ w_in).reshape(h.shape[:2] + (B_N_GROUPS, 3, B_HEADS, B_HEAD_DIM))
    q = _partial_rope(_rmsnorm(z[:, :, :, 0], g_qn[:, None, :]), pos)
    k = _partial_rope(_rmsnorm(z[:, :, :, 1], g_kn[:, None, :]), pos)
    return q, k, z[:, :, :, 2]


def _adaln_kernel(c_ref, w_ref, b_ref, o_ref):
    c = c_ref[...]
    a = (c * jax.nn.sigmoid(c)).astype(BF16)
    o_ref[...] = jnp.dot(a, w_ref[...].astype(BF16), preferred_element_type=F32) + b_ref[...]


def _adaln_all(c_all, w_ada, b_ada):
    n, d = c_all.shape
    depth, _, width = w_ada.shape
    tn = d
    return pl.pallas_call(
        _adaln_kernel,
        grid=(depth, width // tn),
        in_specs=[pl.BlockSpec((n, d), lambda i, j: (0, 0)),
                  pl.BlockSpec((None, d, tn), lambda i, j: (i, 0, j)),
                  pl.BlockSpec((None, 1, tn), lambda i, j: (i, 0, j))],
        out_specs=pl.BlockSpec((None, n, tn), lambda i, j: (i, 0, j)),
        out_shape=jax.ShapeDtypeStruct((depth, n, width), F32),
        compiler_params=_cparams(("parallel", "parallel")),
        name="adaln_modulation",
    )(c_all, w_ada, b_ada.reshape(depth, 1, width))


def _residual_ffn_prologue(x, delta, ga, gn, sh, sc, wr, br, xo_ref, h_ref, lg_ref):
    xn = x + ga * delta
    xo_ref[...] = xn
    y = xn * lax.rsqrt(jnp.mean(xn * xn, axis=-1, keepdims=True) + NORM_EPS) * gn
    hb = (y * (1.0 + sc) + sh).astype(BF16)
    h_ref[...] = hb
    lg_ref[...] = jnp.dot(hb, wr, preferred_element_type=F32) + br


def _epilogue_specs(b, nbb, tm, d):
    row = lambda shape: pl.BlockSpec(shape, lambda i: (i, 0))
    per_batch = pl.BlockSpec((None, 1, d), lambda i: (i // nbb, 0, 0))
    const = lambda shape: pl.BlockSpec(shape, lambda i: (0, 0))
    in_specs = [row((tm, d)), per_batch, const((1, d)), per_batch, per_batch, const((d, ROUTER_LANES)),
                const((1, ROUTER_LANES))]
    out_specs = [row((tm, d)), row((tm, d)), row((tm, ROUTER_LANES))]
    out_shape = [jax.ShapeDtypeStruct((b * nbb * tm, d), F32), jax.ShapeDtypeStruct((b * nbb * tm, d), BF16),
                 jax.ShapeDtypeStruct((b * nbb * tm, ROUTER_LANES), F32)]
    return in_specs, out_specs, out_shape


def _epilogue_args(x, ga, g_ffn, sh2, sc2, w_router, b_router):
    b, s, d = x.shape
    pad = ROUTER_LANES - N_EXPERTS
    return (x.reshape(b * s, d), ga[:, None, :], g_ffn.reshape(1, d), sh2[:, None, :], sc2[:, None, :],
            jnp.pad(w_router, ((0, 0), (0, pad))).astype(BF16), jnp.pad(b_router, (0, pad)).reshape(1, ROUTER_LANES))


def _mla_out_kernel(o_ref, wo_ref, x_ref, ga_ref, gn_ref, sh_ref, sc_ref, wr_ref, br_ref, xo_ref, h_ref, lg_ref):
    delta = jnp.dot(o_ref[...], wo_ref[...], preferred_element_type=F32)
    _residual_ffn_prologue(x_ref[...], delta, ga_ref[...], gn_ref[...], sh_ref[...], sc_ref[...],
                           wr_ref[...], br_ref[...], xo_ref, h_ref, lg_ref)


def _mla_out(o, w_o, x, ga, g_ffn, sh2, sc2, w_router, b_router):
    b, s, d = x.shape
    tm = PROJ_TM
    in_specs, out_specs, out_shape = _epilogue_specs(b, s // tm, tm, d)
    return pl.pallas_call(
        _mla_out_kernel,
        grid=(b * s // tm,),
        in_specs=[pl.BlockSpec((tm, o.shape[1]), lambda i: (i, 0)),
                  pl.BlockSpec(w_o.shape, lambda i: (0, 0))] + in_specs,
        out_specs=out_specs,
        out_shape=out_shape,
        compiler_params=_cparams(("parallel",)),
        name="mla_out_projection",
    )(o, w_o.astype(BF16), *_epilogue_args(x, ga, g_ffn, sh2, sc2, w_router, b_router))


def _mla_proj_kernel(x_ref, sh_ref, sc_ref, gn_ref, win_ref, gq_ref, gkv_ref, wuq_ref, wukv_ref, gqn_ref, gkn_ref,
                     cos_ref, sin_ref, q_ref, k_ref, v_ref, ckv_ref, kpe_ref):
    x = x_ref[...]
    y = x * lax.rsqrt(jnp.mean(x * x, axis=-1, keepdims=True) + NORM_EPS) * gn_ref[...]
    h = (y * (1.0 + sc_ref[...]) + sh_ref[...]).astype(BF16)
    z = jnp.dot(h, win_ref[...], preferred_element_type=F32)
    cos = cos_ref[...]
    sin = sin_ref[...]
    c0 = A_Q_LORA + A_KV_LORA
    zq = z[:, :A_Q_LORA]
    q_lat = zq * lax.rsqrt(jnp.mean(zq * zq, axis=-1, keepdims=True) + NORM_EPS) * gq_ref[...]
    zc = z[:, A_Q_LORA:c0]
    ckv = zc * lax.rsqrt(jnp.mean(zc * zc, axis=-1, keepdims=True) + NORM_EPS) * gkv_ref[...]
    kpe = z[:, c0:c0 + 128] * cos + z[:, c0 + 128:c0 + 256] * sin
    ckv_ref[...] = ckv
    kpe_ref[...] = kpe[:, :A_ROPE]
    qq = jnp.dot(q_lat.astype(BF16), wuq_ref[...], preferred_element_type=F32)
    kv = jnp.dot(ckv.astype(BF16), wukv_ref[...], preferred_element_type=F32)
    kpe_sq = kpe * kpe
    gq_n, gq_p = gqn_ref[:, :A_NOPE], gqn_ref[:, A_NOPE:]
    gk_n, gk_p = gkn_ref[:, :A_NOPE], gkn_ref[:, A_NOPE:]
    rot0 = A_HEADS * A_QPAD
    for hh in range(A_HEADS):
        base = hh * A_QPAD
        qn = qq[:, base:base + A_NOPE]
        qp = qq[:, base + A_NOPE:base + A_QPAD] * cos + qq[:, rot0 + hh * 128:rot0 + (hh + 1) * 128] * sin
        rq = lax.rsqrt(jnp.sum(qn * qn + qp * qp, axis=-1, keepdims=True) * (1.0 / A_DK) + NORM_EPS)
        rq = rq * (A_DK ** -0.5)
        q_ref[:, base:base + A_NOPE] = (qn * rq * gq_n).astype(BF16)
        q_ref[:, base + A_NOPE:base + A_QPAD] = (qp * rq * gq_p).astype(BF16)
        kn = kv[:, hh * (A_NOPE + A_V):hh * (A_NOPE + A_V) + A_NOPE]
        rk = lax.rsqrt(jnp.sum(kn * kn + kpe_sq, axis=-1, keepdims=True) * (1.0 / A_DK) + NORM_EPS)
        k_ref[:, base:base + A_NOPE] = (kn * rk * gk_n).astype(BF16)
        k_ref[:, base + A_NOPE:base + A_QPAD] = (kpe * rk * gk_p).astype(BF16)
        v_ref[:, hh * A_V:(hh + 1) * A_V] = kv[:, hh * (A_NOPE + A_V) + A_NOPE:(hh + 1) * (A_NOPE + A_V)].astype(BF16)


def _rot_half_cols(w):
    half = w.shape[-1] // 2
    return jnp.concatenate([-w[..., half:], w[..., :half]], axis=-1)


def _mla_project_prompt(x, shift, scale, g_norm, pos, w_in, g_q, g_kv, w_uq, g_qn, w_ukv, g_kn):
    b, s, d = x.shape
    tm = PROJ_TM
    nbb = s // tm
    c0 = A_Q_LORA + A_KV_LORA
    zpad = jnp.zeros((d, 128 - A_ROPE), F32)
    w_pe = w_in[:, c0:]
    win = jnp.concatenate([w_in[:, :c0], w_pe, zpad, _rot_half_cols(w_pe), zpad], axis=1).astype(BF16)
    wq3 = w_uq.reshape(A_Q_LORA, A_HEADS, A_DK)
    qz = jnp.zeros((A_Q_LORA, A_HEADS, A_QPAD - A_DK), F32)
    wuq = jnp.concatenate([
        jnp.concatenate([wq3, qz], axis=-1).reshape(A_Q_LORA, A_HEADS * A_QPAD),
        jnp.concatenate([_rot_half_cols(wq3[:, :, A_NOPE:]), qz], axis=-1).reshape(A_Q_LORA, A_HEADS * 128)],
        axis=1).astype(BF16)
    half = A_ROPE // 2
    inv = jnp.exp(-math.log(ROPE_THETA) * jnp.arange(half, dtype=F32) / half)
    ang = pos.astype(F32)[:, None] * inv[None, :]
    tz = jnp.zeros((s, 128 - A_ROPE), F32)
    cos = jnp.concatenate([jnp.cos(ang), jnp.cos(ang), tz], axis=-1)
    sin = jnp.concatenate([jnp.sin(ang), jnp.sin(ang), tz], axis=-1)
    gpad = jnp.zeros((A_QPAD - A_DK,), F32)
    row = lambda shape: pl.BlockSpec(shape, lambda i: (i, 0))
    per_batch = pl.BlockSpec((None, 1, d), lambda i: (i // nbb, 0, 0))
    const = lambda a: pl.BlockSpec(a.shape, lambda i: (0, 0))
    table = pl.BlockSpec((tm, 128), lambda i: (i % nbb, 0))
    consts = [g_norm.reshape(1, d), win, g_q.reshape(1, -1), g_kv.reshape(1, -1), wuq, w_ukv.astype(BF16),
              jnp.concatenate([g_qn, gpad]).reshape(1, A_QPAD), jnp.concatenate([g_kn, gpad]).reshape(1, A_QPAD)]
    return pl.pallas_call(
        _mla_proj_kernel,
        grid=(b * nbb,),
        in_specs=[row((tm, d)), per_batch, per_batch] + [const(a) for a in consts] + [table, table],
        out_specs=[row((tm, A_HEADS * A_QPAD)), row((tm, A_HEADS * A_QPAD)), row((tm, A_HEADS * A_V)),
                   row((tm, A_KV_LORA)), row((tm, A_ROPE))],
        out_shape=[jax.ShapeDtypeStruct((b * s, A_HEADS * A_QPAD), BF16),
                   jax.ShapeDtypeStruct((b * s, A_HEADS * A_QPAD), BF16),
                   jax.ShapeDtypeStruct((b * s, A_HEADS * A_V), BF16),
                   jax.ShapeDtypeStruct((b * s, A_KV_LORA), F32),
                   jax.ShapeDtypeStruct((b * s, A_ROPE), F32)],
        compiler_params=_cparams(("parallel",)),
        name="mla_prompt_projection",
    )(x.reshape(b * s, d), shift[:, None, :], scale[:, None, :], *consts, cos, sin)


def _flash_kernel(q_ref, k_ref, v_ref, o_ref):
    i = pl.program_id(2)
    q = q_ref[...]

    def update(carry, kj, vj, mask):
        m, l, acc = carry
        s = lax.dot_general(q, kj, NT, preferred_element_type=F32)
        if mask is not None:
            s = jnp.where(mask, s, NEG)
        m_new = jnp.maximum(m, jnp.max(s, axis=-1, keepdims=True))
        p = jnp.exp(s - m_new)
        corr = jnp.exp(m - m_new)
        l = l * corr + jnp.sum(p, axis=-1, keepdims=True)
        acc = acc * corr + jnp.dot(p.astype(BF16), vj, preferred_element_type=F32)
        return m_new, l, acc

    def body(j, carry):
        start = pl.multiple_of(j * FLASH_T, FLASH_T)
        return update(carry, k_ref[pl.ds(start, FLASH_T), :], v_ref[pl.ds(start, FLASH_T), :], None)

    init = (jnp.full((FLASH_T, 1), NEG, F32), jnp.zeros((FLASH_T, 1), F32), jnp.zeros((FLASH_T, A_V), F32))
    carry = lax.fori_loop(0, i, body, init)
    start = pl.multiple_of(i * FLASH_T, FLASH_T)
    row = lax.broadcasted_iota(jnp.int32, (FLASH_T, FLASH_T), 0)
    col = lax.broadcasted_iota(jnp.int32, (FLASH_T, FLASH_T), 1)
    m, l, acc = update(carry, k_ref[pl.ds(start, FLASH_T), :], v_ref[pl.ds(start, FLASH_T), :], row >= col)
    o_ref[...] = (acc / l).astype(o_ref.dtype)


def _flash_causal(q, k, v, batch, seq):
    nq = seq // FLASH_T
    return pl.pallas_call(
        _flash_kernel,
        grid=(batch, A_HEADS, nq),
        in_specs=[pl.BlockSpec((FLASH_T, A_QPAD), lambda b, h, i: (b * nq + i, h)),
                  pl.BlockSpec((seq, A_QPAD), lambda b, h, i: (b, h)),
                  pl.BlockSpec((seq, A_V), lambda b, h, i: (b, h))],
        out_specs=pl.BlockSpec((FLASH_T, A_V), lambda b, h, i: (b * nq + i, h)),
        out_shape=jax.ShapeDtypeStruct((batch * seq, A_HEADS * A_V), BF16),
        compiler_params=_cparams(("parallel", "parallel", "arbitrary")),
        name="mla_prompt_flash",
    )(q, k, v)


def _mla_sample_kernel(pt_ref, *refs):
    npg = MLA_PAGES_PER_STEP
    ckv_refs = refs[:npg]
    kpe_refs = refs[npg:2 * npg]
    (qabs_ref, qpe_ref, ckn_ref, kpn_ref, wukt_ref, wuv_ref, o_ref,
     ck_s, kp_s, s_s, m_s, l_s, acc_s) = refs[2 * npg:]
    c = pl.program_id(1)
    nq = qabs_ref.shape[0]

    @pl.when(c == 0)
    def _():
        m_s[...] = jnp.full(m_s.shape, NEG, F32)
        l_s[...] = jnp.zeros(l_s.shape, F32)
        acc_s[...] = jnp.zeros(acc_s.shape, F32)

    qa = qabs_ref[...]
    qp = qpe_ref[...]
    wukt = wukt_ref[...]
    ones = jnp.ones((A_HEADS, A_ROPE), BF16)

    def scores(ck, kp):
        n = ck.shape[0]
        kt = lax.dot_general(wukt, ck, NT, preferred_element_type=F32)
        ssn = jnp.sum((kt * kt).reshape(A_HEADS, A_NOPE, n), axis=1)
        sq = kp * kp
        hi = sq.astype(BF16)
        lo = (sq - hi.astype(F32)).astype(BF16)
        ssp = (lax.dot_general(ones, hi, NT, preferred_element_type=F32)
               + lax.dot_general(ones, lo, NT, preferred_element_type=F32))
        r = lax.rsqrt((ssn + ssp) * (1.0 / A_DK) + NORM_EPS)
        s = (lax.dot_general(qa, ck, NT, preferred_element_type=F32)
             + lax.dot_general(qp, kp.astype(BF16), NT, preferred_element_type=F32))
        return s * jnp.concatenate([r] * (nq // A_HEADS), axis=0)

    def update(s, vals):
        m = m_s[...]
        m_new = jnp.maximum(m, jnp.max(s, axis=-1, keepdims=True))
        p = jnp.exp(s - m_new)
        corr = jnp.exp(m - m_new)
        l_s[...] = l_s[...] * corr + jnp.sum(p, axis=-1, keepdims=True)
        acc_s[...] = acc_s[...] * corr + jnp.dot(p.astype(BF16), vals, preferred_element_type=F32)
        m_s[...] = m_new

    for j in range(npg):
        ck_s[j * PAGE_SIZE:(j + 1) * PAGE_SIZE, :] = ckv_refs[j][...].astype(BF16)
        kp_s[j * PAGE_SIZE:(j + 1) * PAGE_SIZE, :] = kpe_refs[j][...]
    sub = 2 * PAGE_SIZE
    for t in range(npg // 2):
        s_s[:, t * sub:(t + 1) * sub] = scores(ck_s[t * sub:(t + 1) * sub, :], kp_s[t * sub:(t + 1) * sub, :])
    update(s_s[...], ck_s[...])

    @pl.when(c == pl.num_programs(1) - 1)
    def _():
        ckn = ckn_ref[...].astype(BF16)
        s = scores(ckn, kpn_ref[...])
        qi = lax.broadcasted_iota(jnp.int32, s.shape, 0) // A_HEADS
        ti = lax.broadcasted_iota(jnp.int32, s.shape, 1)
        update(jnp.where(ti <= qi, s, NEG), ckn)
        lat = (acc_s[...] / l_s[...]).astype(BF16)
        full = jnp.dot(lat, wuv_ref[...], preferred_element_type=F32)
        hrow = lax.broadcasted_iota(jnp.int32, (A_HEADS, A_HEADS * A_V), 0)
        hcol = lax.broadcasted_iota(jnp.int32, (A_HEADS, A_HEADS * A_V), 1) // A_V
        rows = []
        for qq in range(nq // A_HEADS):
            blk = full[qq * A_HEADS:(qq + 1) * A_HEADS, :]
            rows.append(jnp.sum(jnp.where(hrow == hcol, blk, 0.0), axis=0, keepdims=True))
        o_ref[...] = jnp.concatenate(rows, axis=0)


def _mla_sample_attention(page_table, cache_ckv, cache_kpe, li, qabs, qpe, ckv_new, kpe_new, wukt, wuv):
    db, n_pages = page_table.shape
    ds = qabs.shape[0] // (db * A_HEADS)
    nq = ds * A_HEADS
    npg = MLA_PAGES_PER_STEP
    n_chunks = n_pages // npg
    tok = npg * PAGE_SIZE

    def page_map(j):
        return lambda b, c, pt: (li, pt[b * n_pages + c * npg + j], 0, 0)

    in_specs = ([pl.BlockSpec((None, None, PAGE_SIZE, A_KV_LORA), page_map(j)) for j in range(npg)]
                + [pl.BlockSpec((None, None, PAGE_SIZE, A_ROPE), page_map(j)) for j in range(npg)]
                + [pl.BlockSpec((nq, A_KV_LORA), lambda b, c, pt: (b, 0)),
                   pl.BlockSpec((nq, A_ROPE), lambda b, c, pt: (b, 0)),
                   pl.BlockSpec((None, PAGE_SIZE, A_KV_LORA), lambda b, c, pt: (b, 0, 0)),
                   pl.BlockSpec((None, PAGE_SIZE, A_ROPE), lambda b, c, pt: (b, 0, 0)),
                   pl.BlockSpec((A_HEADS * A_NOPE, A_KV_LORA), lambda b, c, pt: (0, 0)),
                   pl.BlockSpec((A_KV_LORA, A_HEADS * A_V), lambda b, c, pt: (0, 0))])
    grid_spec = pltpu.PrefetchScalarGridSpec(
        num_scalar_prefetch=1,
        grid=(db, n_chunks),
        in_specs=in_specs,
        out_specs=pl.BlockSpec((None, ds, A_HEADS * A_V), lambda b, c, pt: (b, 0, 0)),
        scratch_shapes=[pltpu.VMEM((tok, A_KV_LORA), BF16),
                        pltpu.VMEM((tok, A_ROPE), F32),
                        pltpu.VMEM((nq, tok), F32),
                        pltpu.VMEM((nq, 1), F32),
                        pltpu.VMEM((nq, 1), F32),
                        pltpu.VMEM((nq, A_KV_LORA), F32)])
    return pl.pallas_call(
        _mla_sample_kernel,
        grid_spec=grid_spec,
        out_shape=jax.ShapeDtypeStruct((db, ds, A_HEADS * A_V), F32),
        compiler_params=_cparams(("parallel", "arbitrary")),
        name="mla_sample_paged",
    )(page_table.reshape(-1), *([cache_ckv] * npg), *([cache_kpe] * npg),
      qabs, qpe, ckv_new, kpe_new, wukt, wuv)


def _bproj_kernel(x_ref, sh_ref, sc_ref, gn_ref, w_ref, gq_ref, gk_ref, cos_ref, sin_ref,
                  o1_ref, o2_ref, o3_ref, kv1_ref, kv2_ref, kv3_ref, h_s):
    j = pl.program_id(1)

    @pl.when(j == 0)
    def _():
        x = x_ref[...]
        y = x * lax.rsqrt(jnp.mean(x * x, axis=-1, keepdims=True) + NORM_EPS) * gn_ref[...]
        h_s[...] = (y * (1.0 + sc_ref[...]) + sh_ref[...]).astype(BF16)

    z = jnp.dot(h_s[...], w_ref[...], preferred_element_type=F32)
    lane = lax.broadcasted_iota(jnp.int32, (z.shape[0], B_HEAD_DIM), 1)

    def norm_rope(g_row, scale):
        cos = cos_ref[...]
        sin = sin_ref[...]
        heads = []
        for h in range(B_HEADS):
            zh = z[:, h * B_HEAD_DIM:(h + 1) * B_HEAD_DIM]
            y = zh * lax.rsqrt(jnp.mean(zh * zh, axis=-1, keepdims=True) + NORM_EPS) * g_row
            rot = jnp.where(lane < B_ROT // 2, pltpu.roll(y, B_HEAD_DIM - B_ROT // 2, 1), pltpu.roll(y, B_ROT // 2, 1))
            y = y * cos + rot * sin
            heads.append(y * scale if scale != 1.0 else y)
        return heads

    for g, (o_ref, kv_ref) in enumerate(((o1_ref, kv1_ref), (o2_ref, kv2_ref), (o3_ref, kv3_ref))):
        @pl.when(j == 3 * g)
        def _():
            for h, y in enumerate(norm_rope(gq_ref[g:g + 1, :], B_HEAD_DIM ** -0.5)):
                o_ref[:, h * B_HEAD_DIM:(h + 1) * B_HEAD_DIM] = y.astype(BF16)

        @pl.when(j == 3 * g + 1)
        def _():
            for h, y in enumerate(norm_rope(gk_ref[g:g + 1, :], 1.0)):
                o_ref[:, h * B_HEAD_DIM:(h + 1) * B_HEAD_DIM] = y.astype(BF16)
                kv_ref[:, h * B_HEAD_DIM:(h + 1) * B_HEAD_DIM] = y

        @pl.when(j == 3 * g + 2)
        def _():
            o_ref[...] = z.astype(BF16)
            kv_ref[...] = z


def _rope_tables(pos, half, width):
    inv = jnp.exp(-math.log(ROPE_THETA) * jnp.arange(half, dtype=F32) / half)
    ang = pos.astype(F32)[:, None] * inv[None, :]
    rest = width - 2 * half
    cos = jnp.concatenate([jnp.cos(ang), jnp.cos(ang), jnp.ones((pos.shape[0], rest), F32)], axis=-1)
    sin = jnp.concatenate([-jnp.sin(ang), jnp.sin(ang), jnp.zeros((pos.shape[0], rest), F32)], axis=-1)
    return cos, sin


def _b_project_fused(x, shift, scale, g_norm, w_in, g_qn, g_kn, pos):
    b, s, d = x.shape
    tm = PROJ_TM
    nbb = s // tm
    width = B_HEADS * B_HEAD_DIM
    cos, sin = _rope_tables(pos, B_ROT // 2, B_HEAD_DIM)
    row = lambda i, j: (i, 0)
    per_batch = pl.BlockSpec((None, 1, d), lambda i, j: (i // nbb, 0, 0))
    small = lambda shape: pl.BlockSpec(shape, lambda i, j: (0, 0))
    table = pl.BlockSpec((tm, B_HEAD_DIM), lambda i, j: (i % nbb, 0))
    qkv_specs = [pl.BlockSpec((tm, width), (lambda i, j, g=g: (i, jnp.clip(j - 3 * g, 0, 2)))) for g in range(3)]
    kv_specs = [pl.BlockSpec((tm, width), (lambda i, j, g=g: (i, jnp.clip(j - 3 * g - 1, 0, 1)))) for g in range(3)]
    outs = pl.pallas_call(
        _bproj_kernel,
        grid=(b * nbb, 3 * B_N_GROUPS),
        in_specs=[pl.BlockSpec((tm, d), row), per_batch, per_batch, small((1, d)),
                  pl.BlockSpec((d, width), lambda i, j: (0, j)),
                  small((B_N_GROUPS, B_HEAD_DIM)), small((B_N_GROUPS, B_HEAD_DIM)), table, table],
        out_specs=qkv_specs + kv_specs,
        out_shape=[jax.ShapeDtypeStruct((b * s, 3 * width), BF16)] * 3
        + [jax.ShapeDtypeStruct((b * s, 2 * width), F32)] * 3,
        scratch_shapes=[pltpu.VMEM((tm, d), BF16)],
        compiler_params=_cparams(("parallel", "arbitrary")),
        name="dilated_qkv_projection",
    )(x.reshape(b * s, d), shift[:, None, :], scale[:, None, :], g_norm.reshape(1, d), w_in.astype(BF16),
      g_qn, g_kn, cos, sin)
    return outs[:3], outs[3:]


def _local_kernel(qkv_ref, halo_ref, o_ref, lse_ref, *, has_halo, sub_cols):
    i = pl.program_id(0)
    nsub = LOCAL_T // B_BLOCK
    width = B_HEADS * B_HEAD_DIM
    qi = lax.broadcasted_iota(jnp.int32, (B_BLOCK, 2 * B_BLOCK), 0)
    ki = lax.broadcasted_iota(jnp.int32, (B_BLOCK, 2 * B_BLOCK), 1)
    rel = qi - (ki - B_BLOCK)
    band = (rel >= 0) & (rel <= B_BLOCK)
    causal = (lax.broadcasted_iota(jnp.int32, (B_BLOCK, B_BLOCK), 0)
              >= lax.broadcasted_iota(jnp.int32, (B_BLOCK, B_BLOCK), 1))
    lane = lax.broadcasted_iota(jnp.int32, (B_BLOCK, 128), 1)
    for j in range(nsub):
        rows = slice(0, B_BLOCK) if sub_cols else slice(j * B_BLOCK, (j + 1) * B_BLOCK)
        col0 = j * 3 * width if sub_cols else 0
        ocol0 = j * width if sub_cols else 0
        if sub_cols:
            mask = None
        elif j == 0:
            mask = band & (ki >= jnp.where(has_halo(i), 0, B_BLOCK))
        else:
            mask = band
        lse_tile = jnp.zeros((B_BLOCK, 128), F32)
        for h in range(B_HEADS):
            kc = slice(col0 + width + h * B_HEAD_DIM, col0 + width + (h + 1) * B_HEAD_DIM)
            vc = slice(col0 + 2 * width + h * B_HEAD_DIM, col0 + 2 * width + (h + 1) * B_HEAD_DIM)
            q = qkv_ref[rows, col0 + h * B_HEAD_DIM:col0 + (h + 1) * B_HEAD_DIM]
            if sub_cols:
                kk, vv = qkv_ref[rows, kc], qkv_ref[rows, vc]
                s = lax.dot_general(q, kk, NT, preferred_element_type=F32)
                s = jnp.where(causal, s, NEG)
            else:
                if j == 0:
                    kk = jnp.concatenate([halo_ref[:, kc], qkv_ref[0:B_BLOCK, kc]], axis=0)
                    vv = jnp.concatenate([halo_ref[:, vc], qkv_ref[0:B_BLOCK, vc]], axis=0)
                else:
                    kk = qkv_ref[(j - 1) * B_BLOCK:(j + 1) * B_BLOCK, kc]
                    vv = qkv_ref[(j - 1) * B_BLOCK:(j + 1) * B_BLOCK, vc]
                s = lax.dot_general(q, kk, NT, preferred_element_type=F32)
                s = jnp.where(mask, s, NEG)
            m = jnp.max(s, axis=-1, keepdims=True)
            p = jnp.exp(s - m)
            l = jnp.sum(p, axis=-1, keepdims=True)
            o = jnp.dot(p.astype(BF16), vv, preferred_element_type=F32) / l
            o_ref[rows, ocol0 + h * B_HEAD_DIM:ocol0 + (h + 1) * B_HEAD_DIM] = o.astype(o_ref.dtype)
            lse_tile = jnp.where(lane == h, m + jnp.log(l), lse_tile)
        lse_ref[rows, j * 128 if sub_cols else 0:(j * 128 if sub_cols else 0) + 128] = lse_tile


def _local_attention(qkv, batch, seq, dilation):
    width = B_HEADS * B_HEAD_DIM
    sub_len = seq // dilation
    nsub = LOCAL_T // B_BLOCK
    rows = batch * sub_len
    view = qkv.reshape(rows, dilation * 3 * width)
    if sub_len >= LOCAL_T:
        per_seq = sub_len // LOCAL_T
        grid = (batch * per_seq, dilation)
        cur = pl.BlockSpec((LOCAL_T, 3 * width), lambda i, r: (i, r))
        halo = pl.BlockSpec((B_BLOCK, 3 * width), lambda i, r: (jnp.maximum(i * nsub - 1, 0), r))
        o_spec = pl.BlockSpec((LOCAL_T, width), lambda i, r: (i, r))
        l_spec = pl.BlockSpec((LOCAL_T, 128), lambda i, r: (i, r))
        kern = functools.partial(_local_kernel, has_halo=lambda i: i % per_seq != 0, sub_cols=False)
    else:
        assert sub_len == B_BLOCK and dilation % nsub == 0
        grid = (batch, dilation // nsub)
        cur = pl.BlockSpec((B_BLOCK, nsub * 3 * width), lambda i, r: (i, r))
        halo = pl.BlockSpec((B_BLOCK, 3 * width), lambda i, r: (i, 0))
        o_spec = pl.BlockSpec((B_BLOCK, nsub * width), lambda i, r: (i, r))
        l_spec = pl.BlockSpec((B_BLOCK, nsub * 128), lambda i, r: (i, r))
        kern = functools.partial(_local_kernel, has_halo=None, sub_cols=True)
    o, lse = pl.pallas_call(
        kern,
        grid=grid,
        in_specs=[cur, halo],
        out_specs=[o_spec, l_spec],
        out_shape=[jax.ShapeDtypeStruct((rows, dilation * width), BF16),
                   jax.ShapeDtypeStruct((rows, dilation * 128), F32)],
        compiler_params=_cparams(("parallel", "parallel")),
        name="dilated_local_attention",
    )(view, view)
    return o.reshape(batch * seq, width), lse.reshape(batch * seq, 128)


def _dil_out_kernel(o1_ref, o2_ref, o3_ref, l1_ref, l2_ref, l3_ref, wo_ref, x_ref, ga_ref, gn_ref, sh_ref, sc_ref,
                    wr_ref, br_ref, xo_ref, h_ref, lg_ref, c_s):
    lses = [l1_ref[...], l2_ref[...], l3_ref[...]]
    m = jnp.maximum(jnp.maximum(lses[0], lses[1]), lses[2])
    ws = [jnp.exp(l - m) for l in lses]
    inv = 1.0 / (ws[0] + ws[1] + ws[2])
    ws = [w * inv for w in ws]
    for h in range(B_HEADS):
        cs = slice(h * B_HEAD_DIM, (h + 1) * B_HEAD_DIM)
        acc = ws[0][:, h:h + 1] * o1_ref[:, cs].astype(F32)
        acc = acc + ws[1][:, h:h + 1] * o2_ref[:, cs].astype(F32)
        acc = acc + ws[2][:, h:h + 1] * o3_ref[:, cs].astype(F32)
        c_s[:, cs] = acc.astype(BF16)
    delta = jnp.dot(c_s[...], wo_ref[...], preferred_element_type=F32)
    _residual_ffn_prologue(x_ref[...], delta, ga_ref[...], gn_ref[...], sh_ref[...], sc_ref[...],
                           wr_ref[...], br_ref[...], xo_ref, h_ref, lg_ref)


def _dilated_out(outs, lses, w_o, x, ga, g_ffn, sh2, sc2, w_router, b_router):
    b, s, d = x.shape
    tm = PROJ_TM
    width = B_HEADS * B_HEAD_DIM
    row = lambda shape: pl.BlockSpec(shape, lambda i: (i, 0))
    in_specs, out_specs, out_shape = _epilogue_specs(b, s // tm, tm, d)
    return pl.pallas_call(
        _dil_out_kernel,
        grid=(b * s // tm,),
        in_specs=[row((tm, width))] * 3 + [row((tm, 128))] * 3 + [pl.BlockSpec((width, d), lambda i: (0, 0))]
        + in_specs,
        out_specs=out_specs,
        out_shape=out_shape,
        scratch_shapes=[pltpu.VMEM((tm, width), BF16)],
        compiler_params=_cparams(("parallel",)),
        name="dilated_out_projection",
    )(*outs, *lses, w_o.astype(BF16), *_epilogue_args(x, ga, g_ffn, sh2, sc2, w_router, b_router))


def _moe_kernel(be_ref, nu_ref, xs_ref, wgu_ref, bgu_ref, wd_ref, bd_ref, ys_ref, wgu_bf, wd_bf):
    i = pl.program_id(0)
    e = be_ref[i]
    prev = be_ref[jnp.maximum(i - 1, 0)]

    @pl.when((i == 0) | (e != prev))
    def _():
        wgu_bf[...] = wgu_ref[...].astype(BF16)
        wd_bf[...] = wd_ref[...].astype(BF16)

    @pl.when(i < nu_ref[0])
    def _():
        x = xs_ref[...]
        acc = jnp.zeros((MOE_TM, D_MODEL), F32) + bd_ref[...]
        for c in range(D_FF // MOE_FF_CHUNK):
            lo = c * MOE_FF_CHUNK
            g = jnp.dot(x, wgu_bf[:, lo:lo + MOE_FF_CHUNK], preferred_element_type=F32)
            g = jnp.minimum(g + bgu_ref[:, lo:lo + MOE_FF_CHUNK], SWIGLU_LIMIT)
            u = jnp.dot(x, wgu_bf[:, D_FF + lo:D_FF + lo + MOE_FF_CHUNK], preferred_element_type=F32)
            u = jnp.clip(u + bgu_ref[:, D_FF + lo:D_FF + lo + MOE_FF_CHUNK], -SWIGLU_LIMIT, SWIGLU_LIMIT)
            a = g * jax.nn.sigmoid(SWIGLU_ALPHA * g) * (u + 1.0)
            acc = acc + jnp.dot(a.astype(BF16), wd_bf[lo:lo + MOE_FF_CHUNK, :], preferred_element_type=F32)
        ys_ref[...] = acc.astype(ys_ref.dtype)

    @pl.when(i >= nu_ref[0])
    def _():
        ys_ref[...] = jnp.zeros(ys_ref.shape, ys_ref.dtype)


def _moe_experts(xs, block_e, n_used, w_gu, b_gu, w_down, b_down):
    n_blocks = xs.shape[0] // MOE_TM
    grid_spec = pltpu.PrefetchScalarGridSpec(
        num_scalar_prefetch=2,
        grid=(n_blocks,),
        in_specs=[pl.BlockSpec((MOE_TM, D_MODEL), lambda i, be, nu: (i, 0)),
                  pl.BlockSpec((None, D_MODEL, 2 * D_FF), lambda i, be, nu: (be[i], 0, 0)),
                  pl.BlockSpec((None, 1, 2 * D_FF), lambda i, be, nu: (be[i], 0, 0)),
                  pl.BlockSpec((None, D_FF, D_MODEL), lambda i, be, nu: (be[i], 0, 0)),
                  pl.BlockSpec((None, 1, D_MODEL), lambda i, be, nu: (be[i], 0, 0))],
        out_specs=pl.BlockSpec((MOE_TM, D_MODEL), lambda i, be, nu: (i, 0)),
        scratch_shapes=[pltpu.VMEM((D_MODEL, 2 * D_FF), BF16),
                        pltpu.VMEM((D_FF, D_MODEL), BF16)])
    return pl.pallas_call(
        _moe_kernel,
        grid_spec=grid_spec,
        out_shape=jax.ShapeDtypeStruct(xs.shape, BF16),
        compiler_params=_cparams(("arbitrary",)),
        name="moe_experts",
    )(block_e, n_used, xs, w_gu, b_gu.reshape(N_EXPERTS, 1, 2 * D_FF), w_down,
      b_down.reshape(N_EXPERTS, 1, D_MODEL))


def _combine_kernel(yg_ref, gates_ref, x_ref, ga_ref, o_ref):
    y = jnp.zeros(x_ref.shape, F32)
    for k in range(TOP_K):
        y = y + gates_ref[:, k:k + 1] * yg_ref[:, k * D_MODEL:(k + 1) * D_MODEL].astype(F32)
    o_ref[...] = x_ref[...] + ga_ref[...] * y


def _moe_combine(yg, gates, x, ga, row_block_offset):
    t = x.shape[0]
    nblk = t // COMBINE_TM
    if ga.ndim == 3:
        per = nblk // ga.shape[0]
        ga_spec = pl.BlockSpec((None, 1, D_MODEL), lambda i: (i // per, 0, 0))
    else:
        ga_spec = pl.BlockSpec((COMBINE_TM, D_MODEL), lambda i: (i, 0))
    return pl.pallas_call(
        _combine_kernel,
        grid=(nblk,),
        in_specs=[pl.BlockSpec((COMBINE_TM, TOP_K * D_MODEL), lambda i: (i + row_block_offset, 0)),
                  pl.BlockSpec((COMBINE_TM, TOP_K), lambda i: (i + row_block_offset, 0)),
                  pl.BlockSpec((COMBINE_TM, D_MODEL), lambda i: (i, 0)),
                  ga_spec],
        out_specs=pl.BlockSpec((COMBINE_TM, D_MODEL), lambda i: (i, 0)),
        out_shape=jax.ShapeDtypeStruct((t, D_MODEL), F32),
        compiler_params=_cparams(("parallel",)),
        name="moe_combine",
    )(yg, gates, x, ga)


def _rank_kernel(e_ref, rank_ref, cnt_ref, tri_s, carry_s):
    i = pl.program_id(0)

    @pl.when(i == 0)
    def _():
        r = lax.broadcasted_iota(jnp.int32, tri_s.shape, 0)
        c = lax.broadcasted_iota(jnp.int32, tri_s.shape, 1)
        tri_s[...] = jnp.where(r <= c, 1.0, 0.0).astype(BF16)
        carry_s[...] = jnp.zeros(carry_s.shape, F32)

    e = e_ref[0]
    expert = lax.broadcasted_iota(jnp.int32, (N_EXPERTS, RANK_T), 0)
    onehot = jnp.where(expert == e, 1.0, 0.0)
    incl = jnp.dot(onehot.astype(BF16), tri_s[...], preferred_element_type=F32)
    rank = jnp.sum(onehot * (incl + carry_s[...]), axis=0, keepdims=True) - 1.0
    rank_ref[0] = rank.astype(jnp.int32)
    carry_s[...] = carry_s[...] + incl[:, RANK_T - 1:RANK_T]
    cnt_ref[...] = jnp.broadcast_to(carry_s[...], cnt_ref.shape).astype(jnp.int32)


def _expert_ranks(e_flat):
    tk = e_flat.shape[0]
    nblk = tk // RANK_T
    rank, cnt = pl.pallas_call(
        _rank_kernel,
        grid=(nblk,),
        in_specs=[pl.BlockSpec((1, 1, RANK_T), lambda i: (i, 0, 0))],
        out_specs=[pl.BlockSpec((1, 1, RANK_T), lambda i: (i, 0, 0)),
                   pl.BlockSpec((N_EXPERTS, 128), lambda i: (0, 0))],
        out_shape=[jax.ShapeDtypeStruct((nblk, 1, RANK_T), jnp.int32),
                   jax.ShapeDtypeStruct((N_EXPERTS, 128), jnp.int32)],
        scratch_shapes=[pltpu.VMEM((RANK_T, RANK_T), BF16), pltpu.VMEM((N_EXPERTS, 1), F32)],
        compiler_params=_cparams(("arbitrary",)),
        name="moe_expert_ranks",
    )(e_flat.reshape(nblk, 1, RANK_T))
    return rank.reshape(tk), cnt[:, 0]


def _moe_dispatch_experts(h, logits, w_gu, b_gu, w_down, b_down):
    t = h.shape[0]
    tk = t * TOP_K
    top_val, top_idx = lax.top_k(logits, TOP_K)
    gates = jax.nn.softmax(top_val, axis=-1)
    e_flat = top_idx.reshape(tk)
    rank, counts = _expert_ranks(e_flat)
    padded = (counts + MOE_TM - 1) // MOE_TM * MOE_TM
    pad_end = jnp.cumsum(padded)
    pad_start = pad_end - padded
    dest = pad_start[e_flat] + rank
    n_blocks = -(-tk // MOE_TM) + N_EXPERTS
    n_used = (pad_end[-1] // MOE_TM).astype(jnp.int32)
    slot_tok = jnp.zeros((n_blocks * MOE_TM,), jnp.int32).at[dest].set(
        jnp.arange(tk, dtype=jnp.int32) // TOP_K, unique_indices=True)
    xs = jnp.concatenate([h, jnp.zeros_like(h)], axis=0)[slot_tok]
    blk = jnp.minimum(jnp.arange(n_blocks, dtype=jnp.int32), n_used - 1) * MOE_TM
    block_e = jnp.sum((pad_end[None, :] <= blk[:, None]).astype(jnp.int32), axis=1)
    block_e = jnp.minimum(block_e, N_EXPERTS - 1)
    ys = _moe_experts(xs, block_e, n_used.reshape(1), w_gu, b_gu, w_down, b_down)
    return ys[dest].reshape(t, TOP_K * D_MODEL), gates


def _mla_prompt(x, shift, scale, g_norm, pos, w_in, g_q, g_kv, w_uq, g_qn, w_ukv, g_kn):
    b, s, _ = x.shape
    q, k, v, ckv, kpe = _mla_project_prompt(x, shift, scale, g_norm, pos, w_in, g_q, g_kv, w_uq, g_qn, w_ukv, g_kn)
    o = _flash_causal(q, k, v, b, s)
    return o, ckv.reshape(b, s, A_KV_LORA), kpe.reshape(b, s, A_ROPE)


def _mla_sample(h, pos, cache_ckv, cache_kpe, li, page_table, w_in, g_q, g_kv, w_uq, g_qn, w_ukv, g_kn, w_o):
    db, ds = h.shape[:2]
    q, ckv, kpe = _mla_queries_latents(h, pos, w_in, g_q, g_kv, w_uq, g_qn)
    w3 = w_ukv.reshape(A_KV_LORA, A_HEADS, A_NOPE + A_V)
    w_uk = w3[:, :, :A_NOPE]
    w_uv = w3[:, :, A_NOPE:].reshape(A_KV_LORA, A_HEADS * A_V)
    qs = q * (g_kn * (A_DK ** -0.5))
    qabs = jnp.einsum('bqhd,chd->bqhc', qs[..., :A_NOPE], w_uk).astype(BF16)
    qabs = qabs.reshape(db * ds * A_HEADS, A_KV_LORA)
    qpe = qs[..., A_NOPE:].astype(BF16).reshape(db * ds * A_HEADS, A_ROPE)
    padn = ((0, 0), (0, PAGE_SIZE - ds), (0, 0))
    wukt = w_uk.transpose(1, 2, 0).reshape(A_HEADS * A_NOPE, A_KV_LORA).astype(BF16)
    o = _mla_sample_attention(page_table, cache_ckv, cache_kpe, li, qabs, qpe,
                              jnp.pad(ckv, padn), jnp.pad(kpe, padn), wukt, w_uv.astype(BF16))
    return o @ w_o, ckv, kpe


def _combine_groups(outs, lses, w_o):
    w = jax.nn.softmax(jnp.stack(lses, axis=0), axis=0)
    o = jnp.sum(w[..., None] * jnp.stack(outs, axis=0), axis=0)
    return o.reshape(o.shape[:2] + (B_HEADS * B_HEAD_DIM,)) @ w_o


def _dilated_prompt(x, shift, scale, g_norm, pos, w_in, g_qn, g_kn):
    b, s, _ = x.shape
    qkvs, kvs = _b_project_fused(x, shift, scale, g_norm, w_in, g_qn, g_kn, pos)
    outs, lses, bufs = [], [], []
    for g, (window, dilation) in enumerate(B_GROUPS):
        o, lse = _local_attention(qkvs[g], b, s, dilation)
        outs.append(o)
        lses.append(lse)
        keep = min(window, s)
        bufs.append(kvs[g].reshape(b, s, 2, B_HEADS, B_HEAD_DIM)[:, s - keep:])
    return outs, lses, bufs


def _swa_sample_kernel(q_ref, new_ref, b1_ref, b2_ref, b3_ref, o_ref, lse_ref):
    ds = q_ref.shape[1]
    kvw = 2 * B_HEADS * B_HEAD_DIM
    voff = B_HEADS * B_HEAD_DIM
    row = lax.broadcasted_iota(jnp.int32, (B_BLOCK, 1), 0)
    nrow = lax.broadcasted_iota(jnp.int32, (ds, 1), 0)
    lane = lax.broadcasted_iota(jnp.int32, (1, 128), 1)
    for g, buf_ref in enumerate((b1_ref, b2_ref, b3_ref)):
        dense = B_GROUPS[g][1] == 1
        for j in range(ds):
            base = 0 if dense else j * kvw
            new_ok = (nrow <= j) if dense else (nrow == j)
            lse_row = jnp.zeros((1, 128), F32)
            for h in range(B_HEADS):
                cs = slice(h * B_HEAD_DIM, (h + 1) * B_HEAD_DIM)
                qv = q_ref[g, j:j + 1, cs]
                ks = buf_ref[:, base + h * B_HEAD_DIM:base + (h + 1) * B_HEAD_DIM]
                vs = buf_ref[:, base + voff + h * B_HEAD_DIM:base + voff + (h + 1) * B_HEAD_DIM]
                s_old = jnp.sum(ks * qv, axis=-1, keepdims=True)
                if dense:
                    s_old = jnp.where(row >= j, s_old, NEG)
                s_new = jnp.where(new_ok, jnp.sum(new_ref[g, :, cs] * qv, axis=-1, keepdims=True), NEG)
                m = jnp.maximum(jnp.max(s_old, axis=0, keepdims=True), jnp.max(s_new, axis=0, keepdims=True))
                p_old = jnp.exp(s_old - m)
                p_new = jnp.exp(s_new - m)
                l = jnp.sum(p_old, axis=0, keepdims=True) + jnp.sum(p_new, axis=0, keepdims=True)
                vn = new_ref[g, :, voff + h * B_HEAD_DIM:voff + (h + 1) * B_HEAD_DIM]
                o = jnp.sum(p_old * vs, axis=0, keepdims=True) + jnp.sum(p_new * vn, axis=0, keepdims=True)
                o_ref[g, j:j + 1, cs] = o / l
                lse_row = jnp.where(lane == h, m + jnp.log(l), lse_row)
            lse_ref[g, j:j + 1, :] = lse_row


def _swa_sample_attention(li, q, new_kv, bufs):
    db, ng, ds, width = q.shape
    kvw = 2 * width
    views, specs = [], []
    for g, (window, dilation) in enumerate(B_GROUPS):
        buf = bufs[g]
        assert buf.shape[2] == B_BLOCK * dilation and (dilation == 1 or ds <= dilation)
        views.append(buf.reshape(buf.shape[0], db, B_BLOCK, dilation * kvw))
        cols = kvw if dilation == 1 else ds * kvw
        specs.append(pl.BlockSpec((None, None, B_BLOCK, cols), lambda b: (li, b, 0, 0)))
    return pl.pallas_call(
        _swa_sample_kernel,
        grid=(db,),
        in_specs=[pl.BlockSpec((None, ng, ds, width), lambda b: (b, 0, 0, 0)),
                  pl.BlockSpec((None, ng, ds, kvw), lambda b: (b, 0, 0, 0))] + specs,
        out_specs=[pl.BlockSpec((None, ng, ds, width), lambda b: (b, 0, 0, 0)),
                   pl.BlockSpec((None, ng, ds, 128), lambda b: (b, 0, 0, 0))],
        out_shape=[jax.ShapeDtypeStruct((db, ng, ds, width), F32),
                   jax.ShapeDtypeStruct((db, ng, ds, 128), F32)],
        compiler_params=_cparams(("parallel",)),
        name="swa_sample_attention",
    )(q, new_kv, *views)


def _swa_state_kernel(*refs):
    ng = B_N_GROUPS
    bufs, news, outs, sem = refs[:ng], refs[ng:2 * ng], refs[2 * ng:3 * ng], refs[3 * ng]
    copies = []
    for buf, new, out in zip(bufs, news, outs):
        n_layers, db, wb = buf.shape[:3]
        ds = new.shape[2]
        per = db // SWA_COPY_SPLITS
        for l in range(n_layers):
            for c in range(SWA_COPY_SPLITS):
                sl = pl.ds(c * per, per)
                copies.append((buf.at[l, sl, pl.ds(ds, wb - ds)], out.at[l, sl, pl.ds(0, wb - ds)]))
            copies.append((new.at[l], out.at[l, :, pl.ds(wb - ds, ds)]))
    dmas = [pltpu.make_async_copy(src, dst, sem.at[n]) for n, (src, dst) in enumerate(copies)]
    for d in dmas:
        d.start()
    for d in dmas:
        d.wait()


def _swa_state_update(bufs, news):
    news = [n.reshape(n.shape[:3] + b.shape[3:]) for n, b in zip(news, bufs)]
    n_dma = sum(b.shape[0] * (SWA_COPY_SPLITS + 1) for b in bufs)
    any_spec = pl.BlockSpec(memory_space=pl.ANY)
    return pl.pallas_call(
        _swa_state_kernel,
        in_specs=[any_spec] * (2 * B_N_GROUPS),
        out_specs=[any_spec] * B_N_GROUPS,
        out_shape=[jax.ShapeDtypeStruct(b.shape, b.dtype) for b in bufs],
        scratch_shapes=[pltpu.SemaphoreType.DMA((n_dma,))],
        name="swa_state_shift",
    )(*bufs, *news)


def _dilated_sample(h, pos, li, swa_states, w_in, g_qn, g_kn, w_o):
    db, ds = h.shape[:2]
    q, k, v = _b_project(h, pos, w_in, g_qn, g_kn)
    width = B_HEADS * B_HEAD_DIM
    qs = (q * (B_HEAD_DIM ** -0.5)).reshape(db, ds, B_N_GROUPS, width).transpose(0, 2, 1, 3)
    new_kv = jnp.concatenate([k.reshape(db, ds, B_N_GROUPS, width), v.reshape(db, ds, B_N_GROUPS, width)], axis=-1)
    new_kv = new_kv.transpose(0, 2, 1, 3)
    o, lse = _swa_sample_attention(li, qs, new_kv, swa_states)
    outs = [o[:, g].reshape(db, ds, B_HEADS, B_HEAD_DIM) for g in range(B_N_GROUPS)]
    lses = [lse[:, g, :, :B_HEADS] for g in range(B_N_GROUPS)]
    return _combine_groups(outs, lses, w_o), new_kv


def kernel(x_prompt, x_sample, cache_mla_ckv, cache_mla_kpe, state_swa_kv_w128, state_swa_kv_w512, state_swa_kv_w2048, page_table, c_prompt, c_sample, w_ada, b_ada, g_norm_mix, g_norm_ffn, a_w_in, a_g_q, a_g_kv, a_w_uq, a_w_ukv, a_g_qn, a_g_kn, a_w_o, b_w_in, b_g_qn, b_g_kn, b_w_o, moe_w_router, moe_b_router, moe_w_gu, moe_b_gu, moe_w_down, moe_b_down):
    B, S, D = x_prompt.shape
    DB, DS, _ = x_sample.shape
    past_len = page_table.shape[1] * cache_mla_ckv.shape[2]
    pos_p = jnp.arange(S, dtype=jnp.int32)
    pos_s = past_len + jnp.arange(DS, dtype=jnp.int32)
    swa_states = (state_swa_kv_w128, state_swa_kv_w512, state_swa_kv_w2048)
    x_p, x_s = x_prompt, x_sample
    ckv_p, kpe_p, ckv_s, kpe_s = [], [], [], []
    swa_p, swa_s = [], []
    mods = _adaln_all(jnp.concatenate([c_prompt, c_sample], axis=0), w_ada, b_ada)
    for i in range(DEPTH):
        sh_p, sc_p, ga_p, sh2_p, sc2_p, ga2_p = jnp.split(mods[i, :B], 6, axis=-1)
        sh_s, sc_s, ga_s, sh2_s, sc2_s, ga2_s = jnp.split(mods[i, B:], 6, axis=-1)
        h_s = _modulate(x_s, g_norm_mix[i], sh_s, sc_s)
        li = i // N_MIXERS
        ffn_p = (x_p, ga_p, g_norm_ffn[i], sh2_p, sc2_p, moe_w_router[i], moe_b_router[i])
        if i % N_MIXERS == 0:
            wa = (a_w_in[li], a_g_q[li], a_g_kv[li], a_w_uq[li], a_g_qn[li], a_w_ukv[li], a_g_kn[li])
            o_p, c_p, r_p = _mla_prompt(x_p, sh_p, sc_p, g_norm_mix[i], pos_p, *wa)
            xp2, hp2, lg_p = _mla_out(o_p, a_w_o[li], *ffn_p)
            o_s, c_s, r_s = _mla_sample(h_s, pos_s, cache_mla_ckv, cache_mla_kpe, li, page_table, *wa, a_w_o[li])
            ckv_p.append(c_p)
            kpe_p.append(r_p)
            ckv_s.append(c_s)
            kpe_s.append(r_s)
        else:
            wb = (b_w_in[li], b_g_qn[li], b_g_kn[li])
            outs, lses, bufs_p = _dilated_prompt(x_p, sh_p, sc_p, g_norm_mix[i], pos_p, *wb)
            xp2, hp2, lg_p = _dilated_out(outs, lses, b_w_o[li], *ffn_p)
            o_s, new_kv = _dilated_sample(h_s, pos_s, li, swa_states, *wb, b_w_o[li])
            swa_p.append(bufs_p)
            swa_s.append(new_kv)
        x_s = x_s + ga_s[:, None, :] * o_s
        h2_s = _modulate(x_s, g_norm_ffn[i], sh2_s, sc2_s).reshape(DB * DS, D)
        lg_s = h2_s @ moe_w_router[i] + moe_b_router[i]
        yg, gates = _moe_dispatch_experts(
            jnp.concatenate([hp2, h2_s.astype(BF16)], axis=0),
            jnp.concatenate([lg_p[:, :N_EXPERTS], lg_s], axis=0),
            moe_w_gu[i], moe_b_gu[i], moe_w_down[i], moe_b_down[i])
        x_p = _moe_combine(yg, gates, xp2, ga2_p[:, None, :], 0).reshape(B, S, D)
        x_s = _moe_combine(yg, gates, x_s.reshape(DB * DS, D), jnp.repeat(ga2_s, DS, axis=0),
                           B * S // COMBINE_TM).reshape(DB, DS, D)
    news = [jnp.stack([kv[:, g] for kv in swa_s]) for g in range(B_N_GROUPS)]
    new_states = _swa_state_update(swa_states, news)
    return (x_p, x_s, jnp.stack(ckv_p), jnp.stack(kpe_p), jnp.stack(ckv_s), jnp.stack(kpe_s),
            jnp.stack([b[0] for b in swa_p]), jnp.stack([b[1] for b in swa_p]), jnp.stack([b[2] for b in swa_p]),
            new_states[0], new_states[1], new_states[2])
```

```python
import functools
import math

import jax
import jax.numpy as jnp
from jax import lax
from jax.experimental import pallas as pl
from jax.experimental.pallas import tpu as pltpu

D_MODEL = 1024
DEPTH = 4
PAGE_SIZE = 128
N_MIXERS = 2
ROPE_THETA = 500000.0
NORM_EPS = 1e-6
A_HEADS = 8
A_NOPE = 128
A_ROPE = 64
A_DK = A_NOPE + A_ROPE
A_V = 128
A_Q_LORA = 384
A_KV_LORA = 256
A_QPAD = 256
B_GROUPS = ((128, 1), (512, 4), (2048, 16))
B_N_GROUPS = 3
B_HEADS = 8
B_HEAD_DIM = 128
B_ROT = B_HEAD_DIM // 4
B_BLOCK = 128
N_EXPERTS = 32
TOP_K = 4
D_FF = 1024
SWIGLU_ALPHA = 1.702
SWIGLU_LIMIT = 7.0

F32 = jnp.float32
BF16 = jnp.bfloat16
NEG = -1e30
V7X_VMEM_LIMIT = 56 * 1024 * 1024
MOE_TM = 512
MOE_FF_CHUNK = 512
ROUTER_LANES = 128
RANK_T = 1024
COMBINE_TM = 512
PROJ_TM = 512
SWA_COPY_ROWS = 512
FLASH_T = 512
LOCAL_T = 512
MLA_PAGES_PER_STEP = 16
MLA_SUB_PAGES = 8
NT = (((1,), (1,)), ((), ()))


def _cparams(sem):
    return pltpu.CompilerParams(dimension_semantics=sem, vmem_limit_bytes=V7X_VMEM_LIMIT)


def _rmsnorm(x, g):
    y = x * lax.rsqrt(jnp.mean(x * x, axis=-1, keepdims=True) + NORM_EPS)
    return y * g


def _rope(x, pos):
    half = x.shape[-1] // 2
    inv = jnp.exp(-math.log(ROPE_THETA) * jnp.arange(half, dtype=F32) / half)
    ang = pos.astype(F32)[:, None] * inv[None, :]
    shape = (1, pos.shape[0]) + (1,) * (x.ndim - 3) + (half,)
    cos = jnp.cos(ang).reshape(shape)
    sin = jnp.sin(ang).reshape(shape)
    x1, x2 = x[..., :half], x[..., half:]
    return jnp.concatenate([x1 * cos - x2 * sin, x2 * cos + x1 * sin], axis=-1)


def _partial_rope(x, pos):
    return jnp.concatenate([_rope(x[..., :B_ROT], pos), x[..., B_ROT:]], axis=-1)


def _modulate(x, g, shift, scale):
    return _rmsnorm(x, g) * (1 + scale[:, None, :]) + shift[:, None, :]


def _mla_queries_latents(h, pos, w_in, g_q, g_kv, w_uq, g_qn):
    z = h @ w_in
    q_lat = _rmsnorm(z[..., :A_Q_LORA], g_q)
    ckv = _rmsnorm(z[..., A_Q_LORA:A_Q_LORA + A_KV_LORA], g_kv)
    kpe = _rope(z[..., A_Q_LORA + A_KV_LORA:], pos)
    q = (q_lat @ w_uq).reshape(h.shape[:2] + (A_HEADS, A_DK))
    q = jnp.concatenate([q[..., :A_NOPE], _rope(q[..., A_NOPE:], pos)], axis=-1)
    return _rmsnorm(q, g_qn), ckv, kpe


def _mla_keys_values(ckv, kpe, w_ukv, g_kn):
    kv = (ckv @ w_ukv).reshape(ckv.shape[:-1] + (A_HEADS, A_NOPE + A_V))
    kpe_h = jnp.broadcast_to(kpe[..., None, :], kpe.shape[:-1] + (A_HEADS, A_ROPE))
    k = _rmsnorm(jnp.concatenate([kv[..., :A_NOPE], kpe_h], axis=-1), g_kn)
    return k, kv[..., A_NOPE:]


def _b_project(h, pos, w_in, g_qn, g_kn):
    z = (h @ w_in).reshape(h.shape[:2] + (B_N_GROUPS, 3, B_HEADS, B_HEAD_DIM))
    q = _partial_rope(_rmsnorm(z[:, :, :, 0], g_qn[:, None, :]), pos)
    k = _partial_rope(_rmsnorm(z[:, :, :, 1], g_kn[:, None, :]), pos)
    return q, k, z[:, :, :, 2]


def _adaln_kernel(c_ref, w_ref, b_ref, o_ref):
    c = c_ref[...]
    a = (c * jax.nn.sigmoid(c)).astype(BF16)
    o_ref[...] = jnp.dot(a, w_ref[...].astype(BF16), preferred_element_type=F32) + b_ref[...]


def _adaln_all(c_all, w_ada, b_ada):
    n, d = c_all.shape
    depth, _, width = w_ada.shape
    tn = d
    return pl.pallas_call(
        _adaln_kernel,
        grid=(depth, width // tn),
        in_specs=[pl.BlockSpec((n, d), lambda i, j: (0, 0)),
                  pl.BlockSpec((None, d, tn), lambda i, j: (i, 0, j)),
                  pl.BlockSpec((None, 1, tn), lambda i, j: (i, 0, j))],
        out_specs=pl.BlockSpec((None, n, tn), lambda i, j: (i, 0, j)),
        out_shape=jax.ShapeDtypeStruct((depth, n, width), F32),
        compiler_params=_cparams(("parallel", "parallel")),
        name="adaln_modulation",
    )(c_all, w_ada, b_ada.reshape(depth, 1, width))


def _residual_ffn_prologue(x, delta, ga, gn, sh, sc, wr, br, xo_ref, h_ref, lg_ref):
    xn = x + ga * delta
    xo_ref[...] = xn
    y = xn * lax.rsqrt(jnp.mean(xn * xn, axis=-1, keepdims=True) + NORM_EPS) * gn
    hb = (y * (1.0 + sc) + sh).astype(BF16)
    h_ref[...] = hb
    lg_ref[...] = jnp.dot(hb, wr, preferred_element_type=F32) + br


def _epilogue_specs(b, nbb, tm, d):
    row = lambda shape: pl.BlockSpec(shape, lambda i: (i, 0))
    per_batch = pl.BlockSpec((None, 1, d), lambda i: (i // nbb, 0, 0))
    const = lambda shape: pl.BlockSpec(shape, lambda i: (0, 0))
    in_specs = [row((tm, d)), per_batch, const((1, d)), per_batch, per_batch, const((d, ROUTER_LANES)),
                const((1, ROUTER_LANES))]
    out_specs = [row((tm, d)), row((tm, d)), row((tm, ROUTER_LANES))]
    out_shape = [jax.ShapeDtypeStruct((b * nbb * tm, d), F32), jax.ShapeDtypeStruct((b * nbb * tm, d), BF16),
                 jax.ShapeDtypeStruct((b * nbb * tm, ROUTER_LANES), F32)]
    return in_specs, out_specs, out_shape


def _epilogue_args(x, ga, g_ffn, sh2, sc2, w_router, b_router):
    b, s, d = x.shape
    pad = ROUTER_LANES - N_EXPERTS
    return (x.reshape(b * s, d), ga[:, None, :], g_ffn.reshape(1, d), sh2[:, None, :], sc2[:, None, :],
            jnp.pad(w_router, ((0, 0), (0, pad))).astype(BF16), jnp.pad(b_router, (0, pad)).reshape(1, ROUTER_LANES))


def _mla_out_kernel(o_ref, wo_ref, x_ref, ga_ref, gn_ref, sh_ref, sc_ref, wr_ref, br_ref, xo_ref, h_ref, lg_ref):
    delta = jnp.dot(o_ref[...], wo_ref[...], preferred_element_type=F32)
    _residual_ffn_prologue(x_ref[...], delta, ga_ref[...], gn_ref[...], sh_ref[...], sc_ref[...],
                           wr_ref[...], br_ref[...], xo_ref, h_ref, lg_ref)


def _mla_out(o, w_o, x, ga, g_ffn, sh2, sc2, w_router, b_router):
    b, s, d = x.shape
    tm = PROJ_TM
    in_specs, out_specs, out_shape = _epilogue_specs(b, s // tm, tm, d)
    return pl.pallas_call(
        _mla_out_kernel,
        grid=(b * s // tm,),
        in_specs=[pl.BlockSpec((tm, o.shape[1]), lambda i: (i, 0)),
                  pl.BlockSpec(w_o.shape, lambda i: (0, 0))] + in_specs,
        out_specs=out_specs,
        out_shape=out_shape,
        compiler_params=_cparams(("parallel",)),
        name="mla_out_projection",
    )(o, w_o.astype(BF16), *_epilogue_args(x, ga, g_ffn, sh2, sc2, w_router, b_router))


def _mla_proj_kernel(x_ref, sh_ref, sc_ref, gn_ref, win_ref, gq_ref, gkv_ref, wuq_ref, wukv_ref, gqn_ref, gkn_ref,
                     cos_ref, sin_ref, q_ref, k_ref, v_ref, ckv_ref, kpe_ref):
    x = x_ref[...]
    y = x * lax.rsqrt(jnp.mean(x * x, axis=-1, keepdims=True) + NORM_EPS) * gn_ref[...]
    h = (y * (1.0 + sc_ref[...]) + sh_ref[...]).astype(BF16)
    z = jnp.dot(h, win_ref[...], preferred_element_type=F32)
    cos = cos_ref[...]
    sin = sin_ref[...]
    c0 = A_Q_LORA + A_KV_LORA
    zq = z[:, :A_Q_LORA]
    q_lat = zq * lax.rsqrt(jnp.mean(zq * zq, axis=-1, keepdims=True) + NORM_EPS) * gq_ref[...]
    zc = z[:, A_Q_LORA:c0]
    ckv = zc * lax.rsqrt(jnp.mean(zc * zc, axis=-1, keepdims=True) + NORM_EPS) * gkv_ref[...]
    kpe = z[:, c0:c0 + 128] * cos + z[:, c0 + 128:c0 + 256] * sin
    ckv_ref[...] = ckv
    kpe_ref[...] = kpe[:, :A_ROPE]
    qq = jnp.dot(q_lat.astype(BF16), wuq_ref[...], preferred_element_type=F32)
    kv = jnp.dot(ckv.astype(BF16), wukv_ref[...], preferred_element_type=F32)
    kpe_sq = kpe * kpe
    gq_n, gq_p = gqn_ref[:, :A_NOPE], gqn_ref[:, A_NOPE:]
    gk_n, gk_p = gkn_ref[:, :A_NOPE], gkn_ref[:, A_NOPE:]
    rot0 = A_HEADS * A_QPAD
    for hh in range(A_HEADS):
        base = hh * A_QPAD
        qn = qq[:, base:base + A_NOPE]
        qp = qq[:, base + A_NOPE:base + A_QPAD] * cos + qq[:, rot0 + hh * 128:rot0 + (hh + 1) * 128] * sin
        rq = lax.rsqrt(jnp.sum(qn * qn + qp * qp, axis=-1, keepdims=True) * (1.0 / A_DK) + NORM_EPS)
        rq = rq * (A_DK ** -0.5)
        q_ref[:, base:base + A_NOPE] = (qn * rq * gq_n).astype(BF16)
        q_ref[:, base + A_NOPE:base + A_QPAD] = (qp * rq * gq_p).astype(BF16)
        kn = kv[:, hh * (A_NOPE + A_V):hh * (A_NOPE + A_V) + A_NOPE]
        rk = lax.rsqrt(jnp.sum(kn * kn + kpe_sq, axis=-1, keepdims=True) * (1.0 / A_DK) + NORM_EPS)
        k_ref[:, base:base + A_NOPE] = (kn * rk * gk_n).astype(BF16)
        k_ref[:, base + A_NOPE:base + A_QPAD] = (kpe * rk * gk_p).astype(BF16)
        v_ref[:, hh * A_V:(hh + 1) * A_V] = kv[:, hh * (A_NOPE + A_V) + A_NOPE:(hh + 1) * (A_NOPE + A_V)].astype(BF16)


def _rot_half_cols(w):
    half = w.shape[-1] // 2
    return jnp.concatenate([-w[..., half:], w[..., :half]], axis=-1)


def _mla_project_prompt(x, shift, scale, g_norm, pos, w_in, g_q, g_kv, w_uq, g_qn, w_ukv, g_kn):
    b, s, d = x.shape
    tm = PROJ_TM
    nbb = s // tm
    c0 = A_Q_LORA + A_KV_LORA
    zpad = jnp.zeros((d, 128 - A_ROPE), F32)
    w_pe = w_in[:, c0:]
    win = jnp.concatenate([w_in[:, :c0], w_pe, zpad, _rot_half_cols(w_pe), zpad], axis=1).astype(BF16)
    wq3 = w_uq.reshape(A_Q_LORA, A_HEADS, A_DK)
    qz = jnp.zeros((A_Q_LORA, A_HEADS, A_QPAD - A_DK), F32)
    wuq = jnp.concatenate([
        jnp.concatenate([wq3, qz], axis=-1).reshape(A_Q_LORA, A_HEADS * A_QPAD),
        jnp.concatenate([_rot_half_cols(wq3[:, :, A_NOPE:]), qz], axis=-1).reshape(A_Q_LORA, A_HEADS * 128)],
        axis=1).astype(BF16)
    half = A_ROPE // 2
    inv = jnp.exp(-math.log(ROPE_THETA) * jnp.arange(half, dtype=F32) / half)
    ang = pos.astype(F32)[:, None] * inv[None, :]
    tz = jnp.zeros((s, 128 - A_ROPE), F32)
    cos = jnp.concatenate([jnp.cos(ang), jnp.cos(ang), tz], axis=-1)
    sin = jnp.concatenate([jnp.sin(ang), jnp.sin(ang), tz], axis=-1)
    gpad = jnp.zeros((A_QPAD - A_DK,), F32)
    row = lambda shape: pl.BlockSpec(shape, lambda i: (i, 0))
    per_batch = pl.BlockSpec((None, 1, d), lambda i: (i // nbb, 0, 0))
    const = lambda a: pl.BlockSpec(a.shape, lambda i: (0, 0))
    table = pl.BlockSpec((tm, 128), lambda i: (i % nbb, 0))
    consts = [g_norm.reshape(1, d), win, g_q.reshape(1, -1), g_kv.reshape(1, -1), wuq, w_ukv.astype(BF16),
              jnp.concatenate([g_qn, gpad]).reshape(1, A_QPAD), jnp.concatenate([g_kn, gpad]).reshape(1, A_QPAD)]
    return pl.pallas_call(
        _mla_proj_kernel,
        grid=(b * nbb,),
        in_specs=[row((tm, d)), per_batch, per_batch] + [const(a) for a in consts] + [table, table],
        out_specs=[row((tm, A_HEADS * A_QPAD)), row((tm, A_HEADS * A_QPAD)), row((tm, A_HEADS * A_V)),
                   row((tm, A_KV_LORA)), row((tm, A_ROPE))],
        out_shape=[jax.ShapeDtypeStruct((b * s, A_HEADS * A_QPAD), BF16),
                   jax.ShapeDtypeStruct((b * s, A_HEADS * A_QPAD), BF16),
                   jax.ShapeDtypeStruct((b * s, A_HEADS * A_V), BF16),
                   jax.ShapeDtypeStruct((b * s, A_KV_LORA), F32),
                   jax.ShapeDtypeStruct((b * s, A_ROPE), F32)],
        compiler_params=_cparams(("parallel",)),
        name="mla_prompt_projection",
    )(x.reshape(b * s, d), shift[:, None, :], scale[:, None, :], *consts, cos, sin)


def _flash_kernel(q_ref, k_ref, v_ref, o_ref):
    i = pl.program_id(2)
    q = q_ref[...]

    def update(carry, kj, vj, mask):
        m, l, acc = carry
        s = lax.dot_general(q, kj, NT, preferred_element_type=F32)
        if mask is not None:
            s = jnp.where(mask, s, NEG)
        m_new = jnp.maximum(m, jnp.max(s, axis=-1, keepdims=True))
        p = jnp.exp(s - m_new)
        corr = jnp.exp(m - m_new)
        l = l * corr + jnp.sum(p, axis=-1, keepdims=True)
        acc = acc * corr + jnp.dot(p.astype(BF16), vj, preferred_element_type=F32)
        return m_new, l, acc

    def body(j, carry):
        start = pl.multiple_of(j * FLASH_T, FLASH_T)
        return update(carry, k_ref[pl.ds(start, FLASH_T), :], v_ref[pl.ds(start, FLASH_T), :], None)

    init = (jnp.full((FLASH_T, 1), NEG, F32), jnp.zeros((FLASH_T, 1), F32), jnp.zeros((FLASH_T, A_V), F32))
    carry = lax.fori_loop(0, i, body, init)
    start = pl.multiple_of(i * FLASH_T, FLASH_T)
    row = lax.broadcasted_iota(jnp.int32, (FLASH_T, FLASH_T), 0)
    col = lax.broadcasted_iota(jnp.int32, (FLASH_T, FLASH_T), 1)
    m, l, acc = update(carry, k_ref[pl.ds(start, FLASH_T), :], v_ref[pl.ds(start, FLASH_T), :], row >= col)
    o_ref[...] = (acc / l).astype(o_ref.dtype)


def _flash_causal(q, k, v, batch, seq):
    nq = seq // FLASH_T
    return pl.pallas_call(
        _flash_kernel,
        grid=(batch, A_HEADS, nq),
        in_specs=[pl.BlockSpec((FLASH_T, A_QPAD), lambda b, h, i: (b * nq + i, h)),
                  pl.BlockSpec((seq, A_QPAD), lambda b, h, i: (b, h)),
                  pl.BlockSpec((seq, A_V), lambda b, h, i: (b, h))],
        out_specs=pl.BlockSpec((FLASH_T, A_V), lambda b, h, i: (b * nq + i, h)),
        out_shape=jax.ShapeDtypeStruct((batch * seq, A_HEADS * A_V), BF16),
        compiler_params=_cparams(("parallel", "parallel", "arbitrary")),
        name="mla_prompt_flash",
    )(q, k, v)


def _mla_sample_kernel(pt_ref, *refs):
    npg = MLA_PAGES_PER_STEP
    ckv_refs = refs[:npg]
    kpe_refs = refs[npg:2 * npg]
    (qabs_ref, qpe_ref, ckn_ref, kpn_ref, wukt_ref, wuv_ref, o_ref,
     ck_s, kp_s, s_s, m_s, l_s, acc_s) = refs[2 * npg:]
    c = pl.program_id(1)
    nq = qabs_ref.shape[0]

    @pl.when(c == 0)
    def _():
        m_s[...] = jnp.full(m_s.shape, NEG, F32)
        l_s[...] = jnp.zeros(l_s.shape, F32)
        acc_s[...] = jnp.zeros(acc_s.shape, F32)

    qa = qabs_ref[...]
    qp = qpe_ref[...]
    wukt = wukt_ref[...]

    def scores(ck, kpt):
        n = ck.shape[0]
        kt = lax.dot_general(wukt, ck, NT, preferred_element_type=F32)
        ssn = jnp.sum((kt * kt).reshape(A_HEADS, A_NOPE, n), axis=1)
        ssp = jnp.sum(kpt * kpt, axis=0, keepdims=True)
        r = lax.rsqrt((ssn + ssp) * (1.0 / A_DK) + NORM_EPS)
        s = (lax.dot_general(qa, ck, NT, preferred_element_type=F32)
             + jnp.dot(qp, kpt.astype(BF16), preferred_element_type=F32))
        return s * jnp.concatenate([r] * (nq // A_HEADS), axis=0)

    def update(s, vals):
        m = m_s[...]
        m_new = jnp.maximum(m, jnp.max(s, axis=-1, keepdims=True))
        p = jnp.exp(s - m_new)
        corr = jnp.exp(m - m_new)
        l_s[...] = l_s[...] * corr + jnp.sum(p, axis=-1, keepdims=True)
        acc_s[...] = acc_s[...] * corr + jnp.dot(p.astype(BF16), vals, preferred_element_type=F32)
        m_s[...] = m_new

    for j in range(npg):
        ck_s[j * PAGE_SIZE:(j + 1) * PAGE_SIZE, :] = ckv_refs[j][...].astype(BF16)
        kp_s[:, j * PAGE_SIZE:(j + 1) * PAGE_SIZE] = kpe_refs[j][...]
    sub = MLA_SUB_PAGES * PAGE_SIZE
    for t in range(npg // MLA_SUB_PAGES):
        s_s[:, t * sub:(t + 1) * sub] = scores(ck_s[t * sub:(t + 1) * sub, :], kp_s[:, t * sub:(t + 1) * sub])
    update(s_s[...], ck_s[...])

    @pl.when(c == pl.num_programs(1) - 1)
    def _():
        ckn = ckn_ref[...].astype(BF16)
        s = scores(ckn, kpn_ref[...])
        qi = lax.broadcasted_iota(jnp.int32, s.shape, 0) // A_HEADS
        ti = lax.broadcasted_iota(jnp.int32, s.shape, 1)
        update(jnp.where(ti <= qi, s, NEG), ckn)
        lat = (acc_s[...] / l_s[...]).astype(BF16)
        full = jnp.dot(lat, wuv_ref[...], preferred_element_type=F32)
        hrow = lax.broadcasted_iota(jnp.int32, (A_HEADS, A_HEADS * A_V), 0)
        hcol = lax.broadcasted_iota(jnp.int32, (A_HEADS, A_HEADS * A_V), 1) // A_V
        rows = []
        for qq in range(nq // A_HEADS):
            blk = full[qq * A_HEADS:(qq + 1) * A_HEADS, :]
            rows.append(jnp.sum(jnp.where(hrow == hcol, blk, 0.0), axis=0, keepdims=True))
        o_ref[...] = jnp.concatenate(rows, axis=0)


def _mla_sample_attention(page_table, cache_ckv, cache_kpe_t, li, qabs, qpe, ckv_new, kpe_new_t, wukt, wuv):
    db, n_pages = page_table.shape
    ds = qabs.shape[0] // (db * A_HEADS)
    nq = ds * A_HEADS
    npg = MLA_PAGES_PER_STEP
    n_chunks = n_pages // npg
    tok = npg * PAGE_SIZE

    def page_map(j):
        return lambda b, c, pt: (li, pt[b * n_pages + c * npg + j], 0, 0)

    in_specs = ([pl.BlockSpec((None, None, PAGE_SIZE, A_KV_LORA), page_map(j)) for j in range(npg)]
                + [pl.BlockSpec((None, None, A_ROPE, PAGE_SIZE), page_map(j)) for j in range(npg)]
                + [pl.BlockSpec((nq, A_KV_LORA), lambda b, c, pt: (b, 0)),
                   pl.BlockSpec((nq, A_ROPE), lambda b, c, pt: (b, 0)),
                   pl.BlockSpec((None, PAGE_SIZE, A_KV_LORA), lambda b, c, pt: (b, 0, 0)),
                   pl.BlockSpec((None, A_ROPE, PAGE_SIZE), lambda b, c, pt: (b, 0, 0)),
                   pl.BlockSpec((A_HEADS * A_NOPE, A_KV_LORA), lambda b, c, pt: (0, 0)),
                   pl.BlockSpec((A_KV_LORA, A_HEADS * A_V), lambda b, c, pt: (0, 0))])
    grid_spec = pltpu.PrefetchScalarGridSpec(
        num_scalar_prefetch=1,
        grid=(db, n_chunks),
        in_specs=in_specs,
        out_specs=pl.BlockSpec((None, ds, A_HEADS * A_V), lambda b, c, pt: (b, 0, 0)),
        scratch_shapes=[pltpu.VMEM((tok, A_KV_LORA), BF16),
                        pltpu.VMEM((A_ROPE, tok), F32),
                        pltpu.VMEM((nq, tok), F32),
                        pltpu.VMEM((nq, 1), F32),
                        pltpu.VMEM((nq, 1), F32),
                        pltpu.VMEM((nq, A_KV_LORA), F32)])
    return pl.pallas_call(
        _mla_sample_kernel,
        grid_spec=grid_spec,
        out_shape=jax.ShapeDtypeStruct((db, ds, A_HEADS * A_V), F32),
        compiler_params=_cparams(("parallel", "arbitrary")),
        name="mla_sample_paged",
    )(page_table.reshape(-1), *([cache_ckv] * npg), *([cache_kpe_t] * npg),
      qabs, qpe, ckv_new, kpe_new_t, wukt, wuv)


def _bproj_kernel(x_ref, sh_ref, sc_ref, gn_ref, w_ref, gq_ref, gk_ref, cos_ref, sin_ref,
                  o1_ref, o2_ref, o3_ref, kv1_ref, kv2_ref, kv3_ref, h_s):
    j = pl.program_id(1)

    @pl.when(j == 0)
    def _():
        x = x_ref[...]
        y = x * lax.rsqrt(jnp.mean(x * x, axis=-1, keepdims=True) + NORM_EPS) * gn_ref[...]
        h_s[...] = (y * (1.0 + sc_ref[...]) + sh_ref[...]).astype(BF16)

    z = jnp.dot(h_s[...], w_ref[...], preferred_element_type=F32)
    lane = lax.broadcasted_iota(jnp.int32, (z.shape[0], B_HEAD_DIM), 1)

    def norm_rope(g_row, scale):
        cos = cos_ref[...]
        sin = sin_ref[...]
        heads = []
        for h in range(B_HEADS):
            zh = z[:, h * B_HEAD_DIM:(h + 1) * B_HEAD_DIM]
            y = zh * lax.rsqrt(jnp.mean(zh * zh, axis=-1, keepdims=True) + NORM_EPS) * g_row
            rot = jnp.where(lane < B_ROT // 2, pltpu.roll(y, B_HEAD_DIM - B_ROT // 2, 1), pltpu.roll(y, B_ROT // 2, 1))
            y = y * cos + rot * sin
            heads.append(y * scale if scale != 1.0 else y)
        return heads

    for g, (o_ref, kv_ref) in enumerate(((o1_ref, kv1_ref), (o2_ref, kv2_ref), (o3_ref, kv3_ref))):
        @pl.when(j == 3 * g)
        def _():
            for h, y in enumerate(norm_rope(gq_ref[g:g + 1, :], B_HEAD_DIM ** -0.5)):
                o_ref[:, h * B_HEAD_DIM:(h + 1) * B_HEAD_DIM] = y.astype(BF16)

        @pl.when(j == 3 * g + 1)
        def _():
            for h, y in enumerate(norm_rope(gk_ref[g:g + 1, :], 1.0)):
                o_ref[:, h * B_HEAD_DIM:(h + 1) * B_HEAD_DIM] = y.astype(BF16)
                kv_ref[:, h * B_HEAD_DIM:(h + 1) * B_HEAD_DIM] = y

        @pl.when(j == 3 * g + 2)
        def _():
            o_ref[...] = z.astype(BF16)
            kv_ref[...] = z


def _rope_tables(pos, half, width):
    inv = jnp.exp(-math.log(ROPE_THETA) * jnp.arange(half, dtype=F32) / half)
    ang = pos.astype(F32)[:, None] * inv[None, :]
    rest = width - 2 * half
    cos = jnp.concatenate([jnp.cos(ang), jnp.cos(ang), jnp.ones((pos.shape[0], rest), F32)], axis=-1)
    sin = jnp.concatenate([-jnp.sin(ang), jnp.sin(ang), jnp.zeros((pos.shape[0], rest), F32)], axis=-1)
    return cos, sin


def _b_project_fused(x, shift, scale, g_norm, w_in, g_qn, g_kn, pos):
    b, s, d = x.shape
    tm = PROJ_TM
    nbb = s // tm
    width = B_HEADS * B_HEAD_DIM
    cos, sin = _rope_tables(pos, B_ROT // 2, B_HEAD_DIM)
    row = lambda i, j: (i, 0)
    per_batch = pl.BlockSpec((None, 1, d), lambda i, j: (i // nbb, 0, 0))
    small = lambda shape: pl.BlockSpec(shape, lambda i, j: (0, 0))
    table = pl.BlockSpec((tm, B_HEAD_DIM), lambda i, j: (i % nbb, 0))
    qkv_specs = [pl.BlockSpec((tm, width), (lambda i, j, g=g: (i, jnp.clip(j - 3 * g, 0, 2)))) for g in range(3)]
    kv_specs = [pl.BlockSpec((tm, width), (lambda i, j, g=g: (i, jnp.clip(j - 3 * g - 1, 0, 1)))) for g in range(3)]
    outs = pl.pallas_call(
        _bproj_kernel,
        grid=(b * nbb, 3 * B_N_GROUPS),
        in_specs=[pl.BlockSpec((tm, d), row), per_batch, per_batch, small((1, d)),
                  pl.BlockSpec((d, width), lambda i, j: (0, j)),
                  small((B_N_GROUPS, B_HEAD_DIM)), small((B_N_GROUPS, B_HEAD_DIM)), table, table],
        out_specs=qkv_specs + kv_specs,
        out_shape=[jax.ShapeDtypeStruct((b * s, 3 * width), BF16)] * 3
        + [jax.ShapeDtypeStruct((b * s, 2 * width), F32)] * 3,
        scratch_shapes=[pltpu.VMEM((tm, d), BF16)],
        compiler_params=_cparams(("parallel", "arbitrary")),
        name="dilated_qkv_projection",
    )(x.reshape(b * s, d), shift[:, None, :], scale[:, None, :], g_norm.reshape(1, d), w_in.astype(BF16),
      g_qn, g_kn, cos, sin)
    return outs[:3], outs[3:]


def _local_kernel(qkv_ref, halo_ref, o_ref, lse_ref, *, has_halo, sub_cols):
    i = pl.program_id(0)
    nsub = LOCAL_T // B_BLOCK
    width = B_HEADS * B_HEAD_DIM
    qi = lax.broadcasted_iota(jnp.int32, (B_BLOCK, 2 * B_BLOCK), 0)
    ki = lax.broadcasted_iota(jnp.int32, (B_BLOCK, 2 * B_BLOCK), 1)
    rel = qi - (ki - B_BLOCK)
    band = (rel >= 0) & (rel <= B_BLOCK)
    causal = (lax.broadcasted_iota(jnp.int32, (B_BLOCK, B_BLOCK), 0)
              >= lax.broadcasted_iota(jnp.int32, (B_BLOCK, B_BLOCK), 1))
    lane = lax.broadcasted_iota(jnp.int32, (B_BLOCK, 128), 1)
    for j in range(nsub):
        rows = slice(0, B_BLOCK) if sub_cols else slice(j * B_BLOCK, (j + 1) * B_BLOCK)
        col0 = j * 3 * width if sub_cols else 0
        ocol0 = j * width if sub_cols else 0
        if sub_cols:
            mask = None
        elif j == 0:
            mask = band & (ki >= jnp.where(has_halo(i), 0, B_BLOCK))
        else:
            mask = band
        lse_tile = jnp.zeros((B_BLOCK, 128), F32)
        for h in range(B_HEADS):
            kc = slice(col0 + width + h * B_HEAD_DIM, col0 + width + (h + 1) * B_HEAD_DIM)
            vc = slice(col0 + 2 * width + h * B_HEAD_DIM, col0 + 2 * width + (h + 1) * B_HEAD_DIM)
            q = qkv_ref[rows, col0 + h * B_HEAD_DIM:col0 + (h + 1) * B_HEAD_DIM]
            if sub_cols:
                kk, vv = qkv_ref[rows, kc], qkv_ref[rows, vc]
                s = lax.dot_general(q, kk, NT, preferred_element_type=F32)
                s = jnp.where(causal, s, NEG)
            else:
                if j == 0:
                    kk = jnp.concatenate([halo_ref[:, kc], qkv_ref[0:B_BLOCK, kc]], axis=0)
                    vv = jnp.concatenate([halo_ref[:, vc], qkv_ref[0:B_BLOCK, vc]], axis=0)
                else:
                    kk = qkv_ref[(j - 1) * B_BLOCK:(j + 1) * B_BLOCK, kc]
                    vv = qkv_ref[(j - 1) * B_BLOCK:(j + 1) * B_BLOCK, vc]
                s = lax.dot_general(q, kk, NT, preferred_element_type=F32)
                s = jnp.where(mask, s, NEG)
            m = jnp.max(s, axis=-1, keepdims=True)
            p = jnp.exp(s - m)
            l = jnp.sum(p, axis=-1, keepdims=True)
            o = jnp.dot(p.astype(BF16), vv, preferred_element_type=F32) / l
            o_ref[rows, ocol0 + h * B_HEAD_DIM:ocol0 + (h + 1) * B_HEAD_DIM] = o.astype(o_ref.dtype)
            lse_tile = jnp.where(lane == h, m + jnp.log(l), lse_tile)
        lse_ref[rows, j * 128 if sub_cols else 0:(j * 128 if sub_cols else 0) + 128] = lse_tile


def _local_attention(qkv, batch, seq, dilation):
    width = B_HEADS * B_HEAD_DIM
    sub_len = seq // dilation
    nsub = LOCAL_T // B_BLOCK
    rows = batch * sub_len
    view = qkv.reshape(rows, dilation * 3 * width)
    if sub_len >= LOCAL_T:
        per_seq = sub_len // LOCAL_T
        grid = (batch * per_seq, dilation)
        cur = pl.BlockSpec((LOCAL_T, 3 * width), lambda i, r: (i, r))
        halo = pl.BlockSpec((B_BLOCK, 3 * width), lambda i, r: (jnp.maximum(i * nsub - 1, 0), r))
        o_spec = pl.BlockSpec((LOCAL_T, width), lambda i, r: (i, r))
        l_spec = pl.BlockSpec((LOCAL_T, 128), lambda i, r: (i, r))
        kern = functools.partial(_local_kernel, has_halo=lambda i: i % per_seq != 0, sub_cols=False)
    else:
        assert sub_len == B_BLOCK and dilation % nsub == 0
        grid = (batch, dilation // nsub)
        cur = pl.BlockSpec((B_BLOCK, nsub * 3 * width), lambda i, r: (i, r))
        halo = pl.BlockSpec((B_BLOCK, 3 * width), lambda i, r: (i, 0))
        o_spec = pl.BlockSpec((B_BLOCK, nsub * width), lambda i, r: (i, r))
        l_spec = pl.BlockSpec((B_BLOCK, nsub * 128), lambda i, r: (i, r))
        kern = functools.partial(_local_kernel, has_halo=None, sub_cols=True)
    o, lse = pl.pallas_call(
        kern,
        grid=grid,
        in_specs=[cur, halo],
        out_specs=[o_spec, l_spec],
        out_shape=[jax.ShapeDtypeStruct((rows, dilation * width), BF16),
                   jax.ShapeDtypeStruct((rows, dilation * 128), F32)],
        compiler_params=_cparams(("parallel", "parallel")),
        name="dilated_local_attention",
    )(view, view)
    return o.reshape(batch * seq, width), lse.reshape(batch * seq, 128)


def _dil_out_kernel(o1_ref, o2_ref, o3_ref, l1_ref, l2_ref, l3_ref, wo_ref, x_ref, ga_ref, gn_ref, sh_ref, sc_ref,
                    wr_ref, br_ref, xo_ref, h_ref, lg_ref, c_s):
    lses = [l1_ref[...], l2_ref[...], l3_ref[...]]
    m = jnp.maximum(jnp.maximum(lses[0], lses[1]), lses[2])
    ws = [jnp.exp(l - m) for l in lses]
    inv = 1.0 / (ws[0] + ws[1] + ws[2])
    ws = [w * inv for w in ws]
    for h in range(B_HEADS):
        cs = slice(h * B_HEAD_DIM, (h + 1) * B_HEAD_DIM)
        acc = ws[0][:, h:h + 1] * o1_ref[:, cs].astype(F32)
        acc = acc + ws[1][:, h:h + 1] * o2_ref[:, cs].astype(F32)
        acc = acc + ws[2][:, h:h + 1] * o3_ref[:, cs].astype(F32)
        c_s[:, cs] = acc.astype(BF16)
    delta = jnp.dot(c_s[...], wo_ref[...], preferred_element_type=F32)
    _residual_ffn_prologue(x_ref[...], delta, ga_ref[...], gn_ref[...], sh_ref[...], sc_ref[...],
                           wr_ref[...], br_ref[...], xo_ref, h_ref, lg_ref)


def _dilated_out(outs, lses, w_o, x, ga, g_ffn, sh2, sc2, w_router, b_router):
    b, s, d = x.shape
    tm = PROJ_TM
    width = B_HEADS * B_HEAD_DIM
    row = lambda shape: pl.BlockSpec(shape, lambda i: (i, 0))
    in_specs, out_specs, out_shape = _epilogue_specs(b, s // tm, tm, d)
    return pl.pallas_call(
        _dil_out_kernel,
        grid=(b * s // tm,),
        in_specs=[row((tm, width))] * 3 + [row((tm, 128))] * 3 + [pl.BlockSpec((width, d), lambda i: (0, 0))]
        + in_specs,
        out_specs=out_specs,
        out_shape=out_shape,
        scratch_shapes=[pltpu.VMEM((tm, width), BF16)],
        compiler_params=_cparams(("parallel",)),
        name="dilated_out_projection",
    )(*outs, *lses, w_o.astype(BF16), *_epilogue_args(x, ga, g_ffn, sh2, sc2, w_router, b_router))


def _moe_kernel(be_ref, nu_ref, xs_ref, wgu_ref, bgu_ref, wd_ref, bd_ref, ys_ref, wgu_bf, wd_bf):
    i = pl.program_id(0)
    e = be_ref[i]
    prev = be_ref[jnp.maximum(i - 1, 0)]

    @pl.when((i == 0) | (e != prev))
    def _():
        wgu_bf[...] = wgu_ref[...].astype(BF16)
        wd_bf[...] = wd_ref[...].astype(BF16)

    @pl.when(i < nu_ref[0])
    def _():
        x = xs_ref[...]
        acc = jnp.zeros((MOE_TM, D_MODEL), F32) + bd_ref[...]
        for c in range(D_FF // MOE_FF_CHUNK):
            lo = c * MOE_FF_CHUNK
            g = jnp.dot(x, wgu_bf[:, lo:lo + MOE_FF_CHUNK], preferred_element_type=F32)
            g = jnp.minimum(g + bgu_ref[:, lo:lo + MOE_FF_CHUNK], SWIGLU_LIMIT)
            u = jnp.dot(x, wgu_bf[:, D_FF + lo:D_FF + lo + MOE_FF_CHUNK], preferred_element_type=F32)
            u = jnp.clip(u + bgu_ref[:, D_FF + lo:D_FF + lo + MOE_FF_CHUNK], -SWIGLU_LIMIT, SWIGLU_LIMIT)
            a = g * jax.nn.sigmoid(SWIGLU_ALPHA * g) * (u + 1.0)
            acc = acc + jnp.dot(a.astype(BF16), wd_bf[lo:lo + MOE_FF_CHUNK, :], preferred_element_type=F32)
        ys_ref[...] = acc.astype(ys_ref.dtype)

    @pl.when(i >= nu_ref[0])
    def _():
        ys_ref[...] = jnp.zeros(ys_ref.shape, ys_ref.dtype)


def _moe_experts(xs, block_e, n_used, layer, w_gu, b_gu, w_down, b_down):
    n_blocks = xs.shape[0] // MOE_TM
    grid_spec = pltpu.PrefetchScalarGridSpec(
        num_scalar_prefetch=2,
        grid=(n_blocks,),
        in_specs=[pl.BlockSpec((MOE_TM, D_MODEL), lambda i, be, nu: (i, 0)),
                  pl.BlockSpec((None, None, D_MODEL, 2 * D_FF), lambda i, be, nu: (layer, be[i], 0, 0)),
                  pl.BlockSpec((None, None, 1, 2 * D_FF), lambda i, be, nu: (layer, be[i], 0, 0)),
                  pl.BlockSpec((None, None, D_FF, D_MODEL), lambda i, be, nu: (layer, be[i], 0, 0)),
                  pl.BlockSpec((None, None, 1, D_MODEL), lambda i, be, nu: (layer, be[i], 0, 0))],
        out_specs=pl.BlockSpec((MOE_TM, D_MODEL), lambda i, be, nu: (i, 0)),
        scratch_shapes=[pltpu.VMEM((D_MODEL, 2 * D_FF), BF16),
                        pltpu.VMEM((D_FF, D_MODEL), BF16)])
    return pl.pallas_call(
        _moe_kernel,
        grid_spec=grid_spec,
        out_shape=jax.ShapeDtypeStruct(xs.shape, BF16),
        compiler_params=_cparams(("arbitrary",)),
        name="moe_experts",
    )(block_e, n_used, xs, w_gu, b_gu.reshape(b_gu.shape[:2] + (1, 2 * D_FF)), w_down,
      b_down.reshape(b_down.shape[:2] + (1, D_MODEL)))


def _combine_kernel(*refs):
    yg_refs = refs[:TOP_K]
    gates_ref, x_ref, ga_ref, o_ref = refs[TOP_K:]
    y = jnp.zeros(x_ref.shape, F32)
    for k in range(TOP_K):
        y = y + gates_ref[:, k:k + 1] * yg_refs[k][...].astype(F32)
    o_ref[...] = x_ref[...] + ga_ref[...] * y


def _moe_combine(yg, gates, x, ga, row_block_offset):
    t = x.shape[0]
    nblk = t // COMBINE_TM
    nblk_all = gates.shape[0] // COMBINE_TM
    if ga.ndim == 3:
        per = nblk // ga.shape[0]
        ga_spec = pl.BlockSpec((None, 1, D_MODEL), lambda i: (i // per, 0, 0))
    else:
        ga_spec = pl.BlockSpec((COMBINE_TM, D_MODEL), lambda i: (i, 0))
    yg_specs = [pl.BlockSpec((COMBINE_TM, D_MODEL), (lambda i, k=k: (k * nblk_all + row_block_offset + i, 0)))
                for k in range(TOP_K)]
    return pl.pallas_call(
        _combine_kernel,
        grid=(nblk,),
        in_specs=yg_specs + [pl.BlockSpec((COMBINE_TM, TOP_K), lambda i: (i + row_block_offset, 0)),
                             pl.BlockSpec((COMBINE_TM, D_MODEL), lambda i: (i, 0)),
                             ga_spec],
        out_specs=pl.BlockSpec((COMBINE_TM, D_MODEL), lambda i: (i, 0)),
        out_shape=jax.ShapeDtypeStruct((t, D_MODEL), F32),
        compiler_params=_cparams(("parallel",)),
        name="moe_combine",
    )(*([yg] * TOP_K), gates, x, ga)


def _rank_kernel(e_ref, rank_ref, cnt_ref, tri_s, carry_s):
    i = pl.program_id(0)

    @pl.when(i == 0)
    def _():
        r = lax.broadcasted_iota(jnp.int32, tri_s.shape, 0)
        c = lax.broadcasted_iota(jnp.int32, tri_s.shape, 1)
        tri_s[...] = jnp.where(r <= c, 1.0, 0.0).astype(BF16)
        carry_s[...] = jnp.zeros(carry_s.shape, F32)

    e = e_ref[0]
    expert = lax.broadcasted_iota(jnp.int32, (N_EXPERTS, RANK_T), 0)
    onehot = jnp.where(expert == e, 1.0, 0.0)
    incl = jnp.dot(onehot.astype(BF16), tri_s[...], preferred_element_type=F32)
    rank = jnp.sum(onehot * (incl + carry_s[...]), axis=0, keepdims=True) - 1.0
    rank_ref[0] = rank.astype(jnp.int32)
    carry_s[...] = carry_s[...] + incl[:, RANK_T - 1:RANK_T]
    cnt_ref[...] = jnp.broadcast_to(carry_s[...], cnt_ref.shape).astype(jnp.int32)


def _expert_ranks(e_flat):
    tk = e_flat.shape[0]
    nblk = tk // RANK_T
    rank, cnt = pl.pallas_call(
        _rank_kernel,
        grid=(nblk,),
        in_specs=[pl.BlockSpec((1, 1, RANK_T), lambda i: (i, 0, 0))],
        out_specs=[pl.BlockSpec((1, 1, RANK_T), lambda i: (i, 0, 0)),
                   pl.BlockSpec((N_EXPERTS, 128), lambda i: (0, 0))],
        out_shape=[jax.ShapeDtypeStruct((nblk, 1, RANK_T), jnp.int32),
                   jax.ShapeDtypeStruct((N_EXPERTS, 128), jnp.int32)],
        scratch_shapes=[pltpu.VMEM((RANK_T, RANK_T), BF16), pltpu.VMEM((N_EXPERTS, 1), F32)],
        compiler_params=_cparams(("arbitrary",)),
        name="moe_expert_ranks",
    )(e_flat.reshape(nblk, 1, RANK_T))
    return rank.reshape(tk), cnt[:, 0]


def _moe_dispatch_experts(h, logits, layer, w_gu, b_gu, w_down, b_down):
    t = h.shape[0]
    tk = t * TOP_K
    top_val, top_idx = lax.top_k(logits, TOP_K)
    gates = jax.nn.softmax(top_val, axis=-1)
    e_flat = top_idx.reshape(tk)
    rank, counts = _expert_ranks(e_flat)
    padded = (counts + MOE_TM - 1) // MOE_TM * MOE_TM
    pad_end = jnp.cumsum(padded)
    pad_start = pad_end - padded
    dest = pad_start[e_flat] + rank
    n_blocks = -(-tk // MOE_TM) + N_EXPERTS
    n_used = (pad_end[-1] // MOE_TM).astype(jnp.int32)
    slot_tok = jnp.zeros((n_blocks * MOE_TM,), jnp.int32).at[dest].set(
        jnp.arange(tk, dtype=jnp.int32) // TOP_K, unique_indices=True)
    xs = jnp.concatenate([h, jnp.zeros_like(h)], axis=0)[slot_tok]
    blk = jnp.minimum(jnp.arange(n_blocks, dtype=jnp.int32), n_used - 1) * MOE_TM
    block_e = jnp.sum((pad_end[None, :] <= blk[:, None]).astype(jnp.int32), axis=1)
    block_e = jnp.minimum(block_e, N_EXPERTS - 1)
    ys = _moe_experts(xs, block_e, n_used.reshape(1), layer, w_gu, b_gu, w_down, b_down)
    return ys[dest.reshape(t, TOP_K).T.reshape(tk)], gates


def _mla_prompt(x, shift, scale, g_norm, pos, w_in, g_q, g_kv, w_uq, g_qn, w_ukv, g_kn):
    b, s, _ = x.shape
    q, k, v, ckv, kpe = _mla_project_prompt(x, shift, scale, g_norm, pos, w_in, g_q, g_kv, w_uq, g_qn, w_ukv, g_kn)
    o = _flash_causal(q, k, v, b, s)
    return o, ckv.reshape(b, s, A_KV_LORA), kpe.reshape(b, s, A_ROPE)


def _mla_sample(h, pos, cache_ckv, cache_kpe, li, page_table, w_in, g_q, g_kv, w_uq, g_qn, w_ukv, g_kn, w_o):
    db, ds = h.shape[:2]
    q, ckv, kpe = _mla_queries_latents(h, pos, w_in, g_q, g_kv, w_uq, g_qn)
    w3 = w_ukv.reshape(A_KV_LORA, A_HEADS, A_NOPE + A_V)
    w_uk = w3[:, :, :A_NOPE]
    w_uv = w3[:, :, A_NOPE:].reshape(A_KV_LORA, A_HEADS * A_V)
    qs = q * (g_kn * (A_DK ** -0.5))
    qabs = jnp.einsum('bqhd,chd->bqhc', qs[..., :A_NOPE], w_uk).astype(BF16)
    qabs = qabs.reshape(db * ds * A_HEADS, A_KV_LORA)
    qpe = qs[..., A_NOPE:].astype(BF16).reshape(db * ds * A_HEADS, A_ROPE)
    padn = ((0, 0), (0, PAGE_SIZE - ds), (0, 0))
    wukt = w_uk.transpose(1, 2, 0).reshape(A_HEADS * A_NOPE, A_KV_LORA).astype(BF16)
    o = _mla_sample_attention(page_table, cache_ckv, cache_kpe.transpose(0, 1, 3, 2), li, qabs, qpe,
                              jnp.pad(ckv, padn), jnp.pad(kpe, padn).transpose(0, 2, 1), wukt, w_uv.astype(BF16))
    return o @ w_o, ckv, kpe


def _combine_groups(outs, lses, w_o):
    w = jax.nn.softmax(jnp.stack(lses, axis=0), axis=0)
    o = jnp.sum(w[..., None] * jnp.stack(outs, axis=0), axis=0)
    return o.reshape(o.shape[:2] + (B_HEADS * B_HEAD_DIM,)) @ w_o


def _dilated_prompt(x, shift, scale, g_norm, pos, w_in, g_qn, g_kn):
    b, s, _ = x.shape
    qkvs, kvs = _b_project_fused(x, shift, scale, g_norm, w_in, g_qn, g_kn, pos)
    outs, lses, bufs = [], [], []
    for g, (window, dilation) in enumerate(B_GROUPS):
        o, lse = _local_attention(qkvs[g], b, s, dilation)
        outs.append(o)
        lses.append(lse)
        keep = min(window, s)
        bufs.append(kvs[g].reshape(b, s, 2, B_HEADS, B_HEAD_DIM)[:, s - keep:])
    return outs, lses, bufs


def _swa_sample_kernel(q_ref, new_ref, b1_ref, b2_ref, b3_ref, o_ref, lse_ref):
    ds = q_ref.shape[1]
    key = lax.broadcasted_iota(jnp.int32, (B_BLOCK, B_HEADS, 1), 0)
    nkey = lax.broadcasted_iota(jnp.int32, (ds, B_HEADS, 1), 0)
    for g, buf_ref in enumerate((b1_ref, b2_ref, b3_ref)):
        dense = B_GROUPS[g][1] == 1
        k_new = new_ref[g, :, 0]
        v_new = new_ref[g, :, 1]
        for j in range(ds):
            r = 0 if dense else j
            qv = q_ref[g, j]
            ks = buf_ref[:, r, 0]
            vs = buf_ref[:, r, 1]
            s_old = jnp.sum(ks * qv, axis=-1, keepdims=True)
            if dense:
                s_old = jnp.where(key >= j, s_old, NEG)
            new_ok = (nkey <= j) if dense else (nkey == j)
            s_new = jnp.where(new_ok, jnp.sum(k_new * qv, axis=-1, keepdims=True), NEG)
            m = jnp.maximum(jnp.max(s_old, axis=0), jnp.max(s_new, axis=0))
            p_old = jnp.exp(s_old - m)
            p_new = jnp.exp(s_new - m)
            l = jnp.sum(p_old, axis=0) + jnp.sum(p_new, axis=0)
            o = jnp.sum(p_old * vs, axis=0) + jnp.sum(p_new * v_new, axis=0)
            o_ref[g, j] = o / l
            lse_ref[g, j] = jnp.broadcast_to(m + jnp.log(l), (B_HEADS, B_HEAD_DIM))


def _swa_sample_attention(li, q, new_kv, bufs):
    db, ng, ds = q.shape[:3]
    tail = (2, B_HEADS, B_HEAD_DIM)
    views, specs = [], []
    for g, (window, dilation) in enumerate(B_GROUPS):
        buf = bufs[g]
        assert buf.shape[2] == B_BLOCK * dilation and (dilation == 1 or ds <= dilation)
        views.append(buf.reshape((buf.shape[0], db, B_BLOCK, dilation) + tail))
        specs.append(pl.BlockSpec((None, None, B_BLOCK, min(ds, dilation)) + tail, lambda b: (li, b, 0, 0, 0, 0, 0)))
    qo_spec = pl.BlockSpec((None, ng, ds, B_HEADS, B_HEAD_DIM), lambda b: (b, 0, 0, 0, 0))
    return pl.pallas_call(
        _swa_sample_kernel,
        grid=(db,),
        in_specs=[qo_spec, pl.BlockSpec((None, ng, ds) + tail, lambda b: (b, 0, 0, 0, 0, 0))] + specs,
        out_specs=[qo_spec, qo_spec],
        out_shape=[jax.ShapeDtypeStruct(q.shape, F32), jax.ShapeDtypeStruct(q.shape, F32)],
        compiler_params=_cparams(("parallel",)),
        name="swa_sample_attention",
    )(q, new_kv, *views)


def _swa_shift_kernel(cur_ref, nxt_ref, new_ref, o_ref):
    rows = cur_ref.shape[1]
    ds = nxt_ref.shape[1]
    o_ref[:, 0:rows - ds] = cur_ref[:, ds:rows]
    last = pl.program_id(2) == pl.num_programs(2) - 1

    @pl.when(last)
    def _():
        o_ref[:, rows - ds:rows] = new_ref[...]

    @pl.when(jnp.logical_not(last))
    def _():
        o_ref[:, rows - ds:rows] = nxt_ref[...]


def _swa_state_update(bufs, news):
    outs = []
    for buf, new in zip(bufs, news):
        n_layers, db, wb = buf.shape[:3]
        ds = new.shape[2]
        tail = buf.shape[3:]
        new = new.reshape(new.shape[:3] + tail)
        rows = min(wb, SWA_COPY_ROWS)
        nseq = SWA_COPY_ROWS // rows
        nblk = wb // rows
        per = rows // ds
        zeros = (0,) * len(tail)
        blk = lambda r: (None, nseq, r) + tail
        outs.append(pl.pallas_call(
            _swa_shift_kernel,
            grid=(n_layers, db // nseq, nblk),
            in_specs=[pl.BlockSpec(blk(rows), lambda l, b, i: (l, b, i) + zeros),
                      pl.BlockSpec(blk(ds), lambda l, b, i: (l, b, jnp.minimum((i + 1) * per, wb // ds - 1)) + zeros),
                      pl.BlockSpec(blk(ds), lambda l, b, i: (l, b, 0) + zeros)],
            out_specs=pl.BlockSpec(blk(rows), lambda l, b, i: (l, b, i) + zeros),
            out_shape=jax.ShapeDtypeStruct(buf.shape, buf.dtype),
            compiler_params=_cparams(("parallel", "parallel", "arbitrary")),
            name="swa_state_shift",
        )(buf, buf, new))
    return outs


def _dilated_sample(h, pos, li, swa_states, w_in, g_qn, g_kn, w_o):
    db, ds = h.shape[:2]
    q, k, v = _b_project(h, pos, w_in, g_qn, g_kn)
    qs = (q * (B_HEAD_DIM ** -0.5)).transpose(0, 2, 1, 3, 4)
    new_kv = jnp.stack([k, v], axis=3).transpose(0, 2, 1, 3, 4, 5)
    o, lse = _swa_sample_attention(li, qs, new_kv, swa_states)
    outs = [o[:, g] for g in range(B_N_GROUPS)]
    lses = [lse[:, g, :, :, 0] for g in range(B_N_GROUPS)]
    return _combine_groups(outs, lses, w_o), new_kv


def kernel(x_prompt, x_sample, cache_mla_ckv, cache_mla_kpe, state_swa_kv_w128, state_swa_kv_w512, state_swa_kv_w2048, page_table, c_prompt, c_sample, w_ada, b_ada, g_norm_mix, g_norm_ffn, a_w_in, a_g_q, a_g_kv, a_w_uq, a_w_ukv, a_g_qn, a_g_kn, a_w_o, b_w_in, b_g_qn, b_g_kn, b_w_o, moe_w_router, moe_b_router, moe_w_gu, moe_b_gu, moe_w_down, moe_b_down):
    B, S, D = x_prompt.shape
    DB, DS, _ = x_sample.shape
    past_len = page_table.shape[1] * cache_mla_ckv.shape[2]
    pos_p = jnp.arange(S, dtype=jnp.int32)
    pos_s = past_len + jnp.arange(DS, dtype=jnp.int32)
    swa_states = (state_swa_kv_w128, state_swa_kv_w512, state_swa_kv_w2048)
    x_p, x_s = x_prompt, x_sample
    ckv_p, kpe_p, ckv_s, kpe_s = [], [], [], []
    swa_p, swa_s = [], []
    mods = _adaln_all(jnp.concatenate([c_prompt, c_sample], axis=0), w_ada, b_ada)
    for i in range(DEPTH):
        sh_p, sc_p, ga_p, sh2_p, sc2_p, ga2_p = jnp.split(mods[i, :B], 6, axis=-1)
        sh_s, sc_s, ga_s, sh2_s, sc2_s, ga2_s = jnp.split(mods[i, B:], 6, axis=-1)
        h_s = _modulate(x_s, g_norm_mix[i], sh_s, sc_s)
        li = i // N_MIXERS
        ffn_p = (x_p, ga_p, g_norm_ffn[i], sh2_p, sc2_p, moe_w_router[i], moe_b_router[i])
        if i % N_MIXERS == 0:
            wa = (a_w_in[li], a_g_q[li], a_g_kv[li], a_w_uq[li], a_g_qn[li], a_w_ukv[li], a_g_kn[li])
            o_p, c_p, r_p = _mla_prompt(x_p, sh_p, sc_p, g_norm_mix[i], pos_p, *wa)
            xp2, hp2, lg_p = _mla_out(o_p, a_w_o[li], *ffn_p)
            o_s, c_s, r_s = _mla_sample(h_s, pos_s, cache_mla_ckv, cache_mla_kpe, li, page_table, *wa, a_w_o[li])
            ckv_p.append(c_p)
            kpe_p.append(r_p)
            ckv_s.append(c_s)
            kpe_s.append(r_s)
        else:
            wb = (b_w_in[li], b_g_qn[li], b_g_kn[li])
            outs, lses, bufs_p = _dilated_prompt(x_p, sh_p, sc_p, g_norm_mix[i], pos_p, *wb)
            xp2, hp2, lg_p = _dilated_out(outs, lses, b_w_o[li], *ffn_p)
            o_s, new_kv = _dilated_sample(h_s, pos_s, li, swa_states, *wb, b_w_o[li])
            swa_p.append(bufs_p)
            swa_s.append(new_kv)
        x_s = x_s + ga_s[:, None, :] * o_s
        h2_s = _modulate(x_s, g_norm_ffn[i], sh2_s, sc2_s).reshape(DB * DS, D)
        lg_s = h2_s @ moe_w_router[i] + moe_b_router[i]
        yg, gates = _moe_dispatch_experts(
            jnp.concatenate([hp2, h2_s.astype(BF16)], axis=0),
            jnp.concatenate([lg_p[:, :N_EXPERTS], lg_s], axis=0),
            i, moe_w_gu, moe_b_gu, moe_w_down, moe_b_down)
        x_p = _moe_combine(yg, gates, xp2, ga2_p[:, None, :], 0).reshape(B, S, D)
        x_s = _moe_combine(yg, gates, x_s.reshape(DB * DS, D), jnp.repeat(ga2_s, DS, axis=0),
                           B * S // COMBINE_TM).reshape(DB, DS, D)
    news = [jnp.stack([kv[:, g] for kv in swa_s]) for g in range(B_N_GROUPS)]
    new_states = _swa_state_update(swa_states, news)
    return (x_p, x_s, jnp.stack(ckv_p), jnp.stack(kpe_p), jnp.stack(ckv_s), jnp.stack(kpe_s),
            jnp.stack([b[0] for b in swa_p]), jnp.stack([b[1] for b in swa_p]), jnp.stack([b[2] for b in swa_p]),
            new_states[0], new_states[1], new_states[2])
```
